```python
import math
import jax, jax.numpy as jnp
from jax import lax
import numpy as np

D_MODEL = 1024
BATCH = 8
SEQ = 4096
DEPTH = 4

N_MIXERS = 2
N_A_LAYERS = (DEPTH + 1) // 2
N_B_LAYERS = DEPTH // 2
N_VRES = max(N_B_LAYERS - 1, 0)

D_FF = 2816
CHUNK = 128
D_SGU = 2 * D_MODEL
SGU_GROUPS = 16
SGU_GROUP_DIM = D_SGU // SGU_GROUPS
RWKV_HEAD = 64
RWKV_HEADS = D_MODEL // RWKV_HEAD
DECAY_LORA = 64
AAA_LORA = 64
MV_LORA = 32
GATE_LORA = 160
GN_EPS = 64e-5
LN_EPS = 1e-5
DN_ALPHA = (2 * DEPTH) ** 0.25
DN_BETA = (8 * DEPTH) ** -0.25

kernel_name = "hybrid_sgu_rwkv7_macaron_deepnorm"


def layer_norm(x, g, b, eps=LN_EPS):
    xf = x.astype(jnp.float32)
    mu = jnp.mean(xf, axis=-1, keepdims=True)
    var = jnp.mean(jnp.square(xf - mu), axis=-1, keepdims=True)
    return ((xf - mu) * lax.rsqrt(var + eps) * g + b).astype(x.dtype)


def swiglu(x, w_in, w_out):
    gate, up = jnp.split(x @ w_in, 2, axis=-1)
    return (jax.nn.silu(gate) * up) @ w_out


def sgu_mixer(x, w_in, b_in, ln_g, ln_b, w_sp, b_sp, w_out):
    bsz, seq, _ = x.shape
    z = jax.nn.gelu(x @ w_in + b_in, approximate=False)
    u, v = jnp.split(z, 2, axis=-1)
    v = layer_norm(v, ln_g, ln_b)
    n_chunks = seq // CHUNK
    v = v.reshape(bsz, n_chunks, CHUNK, SGU_GROUPS, SGU_GROUP_DIM)
    causal = jnp.tril(jnp.ones((CHUNK, CHUNK), dtype=bool))
    w = jnp.where(causal[None], w_sp, 0)
    mixed = jnp.einsum('gts,bcsgd->bctgd', w, v) + b_sp.T[None, None, :, :, None]
    mixed = mixed.reshape(bsz, seq, D_SGU)
    return (u * mixed) @ w_out


def rwkv7_scan(r, decay, k, v, a, b):
    def step(state, inp):
        r_t, w_t, k_t, v_t, a_t, b_t = inp
        sa = jnp.einsum('bhij,bhj->bhi', state, a_t)
        state = (state * w_t[:, :, None, :] + sa[..., None] * b_t[:, :, None, :]
                 + v_t[..., None] * k_t[:, :, None, :])
        y = jnp.einsum('bhij,bhj->bhi', state, r_t)
        return state, y
    s0 = jnp.zeros(r.shape[1:3] + (RWKV_HEAD, RWKV_HEAD), jnp.float32)
    _, y = lax.scan(step, s0, (r, decay, k, v, a, b))
    return y


def rwkv7_mixer(x, v_first, vres, mu, w_rkv, w0, w1, w2, a0, a1, a2,
                g1, g2, k_k, k_a, r_k, gn_g, gn_b, w_o):
    bsz, seq, d = x.shape
    H, N = RWKV_HEADS, RWKV_HEAD
    xx = jnp.pad(x, ((0, 0), (1, 0), (0, 0)))[:, :-1] - x
    x_rkv = x[None] + xx[None] * mu[:3, None, None, :]
    r, k, v = jnp.einsum('nbsd,nde->nbse', x_rkv, w_rkv)
    xw = x + xx * mu[3]
    xa = x + xx * mu[4]
    xg = x + xx * mu[5]
    w = -jax.nn.softplus(-(w0 + jnp.tanh(xw @ w1) @ w2)) - 0.5
    if vres is None:
        v_first = v
    else:
        v0, v1, v2 = vres
        xv = x + xx * mu[2]
        v = v + (v_first - v) * jax.nn.sigmoid(v0 + (xv @ v1) @ v2)
    a = jax.nn.sigmoid(a0 + (xa @ a1) @ a2)
    g = jax.nn.sigmoid(xg @ g1) @ g2
    kk = (k * k_k).reshape(bsz, seq, H, N).astype(jnp.float32)
    kk = kk / jnp.maximum(jnp.linalg.norm(kk, axis=-1, keepdims=True), 1e-12)
    k = k * (1 + (a - 1) * k_a)

    def heads(z):
        return z.reshape(bsz, seq, H, N).astype(jnp.float32)

    def tmaj(z):
        return jnp.moveaxis(z, 1, 0)

    r_h, k_h, v_h, a_h = heads(r), heads(k), heads(v), heads(a)
    decay = jnp.exp(-jnp.exp(heads(w)))
    y = rwkv7_scan(tmaj(r_h), tmaj(decay), tmaj(k_h), tmaj(v_h), tmaj(-kk), tmaj(kk * a_h))
    y = jnp.moveaxis(y, 0, 1)
    mean = jnp.mean(y, axis=-1, keepdims=True)
    var = jnp.mean(jnp.square(y - mean), axis=-1, keepdims=True)
    y = (y - mean) * lax.rsqrt(var + GN_EPS) * gn_g.reshape(H, N) + gn_b.reshape(H, N)
    bonus = jnp.sum(r_h * k_h * r_k, axis=-1, keepdims=True) * v_h
    out = ((y + bonus).reshape(bsz, seq, d).astype(x.dtype) * g) @ w_o
    return out, v_first


def setup_inputs(seed: int = 0) -> dict:
    key = jax.random.key(seed)
    ks = iter(jax.random.split(key, 40))
    f32 = jnp.float32

    def nrm(shape, scale):
        return jax.random.normal(next(ks), shape, f32) * scale

    def unif(shape, lo, hi):
        return jax.random.uniform(next(ks), shape, f32, lo, hi)

    D = D_MODEL
    NA, NB, NV = N_A_LAYERS, N_B_LAYERS, N_VRES
    return {
        "x": nrm((BATCH, SEQ, D), 1.0),
        "ln_g": 1.0 + nrm((DEPTH, 3, D), 0.02),
        "ln_b": nrm((DEPTH, 3, D), 0.02),
        "ffn_w_in": nrm((DEPTH, 2, D, 2 * D_FF), D ** -0.5),
        "ffn_w_out": nrm((DEPTH, 2, D_FF, D), D_FF ** -0.5 * DN_BETA),
        "sgu_w_in": nrm((NA, D, 2 * D_SGU), D ** -0.5),
        "sgu_b_in": nrm((NA, 2 * D_SGU), 0.02),
        "sgu_ln_g": 1.0 + nrm((NA, D_SGU), 0.02),
        "sgu_ln_b": nrm((NA, D_SGU), 0.02),
        "sgu_w_sp": nrm((NA, SGU_GROUPS, CHUNK, CHUNK), CHUNK ** -0.5),
        "sgu_b_sp": 1.0 + nrm((NA, SGU_GROUPS, CHUNK), 0.02),
        "sgu_w_out": nrm((NA, D_SGU, D), D_SGU ** -0.5 * DN_BETA),
        "rwkv_mu": unif((NB, 6, D), 0.0, 1.0),
        "rwkv_w_rkv": nrm((NB, 3, D, D), D ** -0.5),
        "rwkv_w0": unif((NB, D), -6.0, -1.0),
        "rwkv_w1": nrm((NB, D, DECAY_LORA), D ** -0.5),
        "rwkv_w2": nrm((NB, DECAY_LORA, D), 0.1 * DECAY_LORA ** -0.5),
        "rwkv_a0": nrm((NB, D), 0.1),
        "rwkv_a1": nrm((NB, D, AAA_LORA), D ** -0.5),
        "rwkv_a2": nrm((NB, AAA_LORA, D), 0.5 * AAA_LORA ** -0.5),
        "rwkv_v0": nrm((NV, D), 0.1),
        "rwkv_v1": nrm((NV, D, MV_LORA), D ** -0.5),
        "rwkv_v2": nrm((NV, MV_LORA, D), 0.5 * MV_LORA ** -0.5),
        "rwkv_g1": nrm((NB, D, GATE_LORA), D ** -0.5),
        "rwkv_g2": nrm((NB, GATE_LORA, D), GATE_LORA ** -0.5),
        "rwkv_k_k": 0.85 + nrm((NB, D), 0.02),
        "rwkv_k_a": 1.0 + nrm((NB, D), 0.02),
        "rwkv_r_k": nrm((NB, RWKV_HEADS, RWKV_HEAD), 0.1),
        "rwkv_gn_g": 1.0 + nrm((NB, D), 0.02),
        "rwkv_gn_b": nrm((NB, D), 0.02),
        "rwkv_w_o": nrm((NB, D, D), D ** -0.5 * DN_BETA),
    }


def reference(x, ln_g, ln_b, ffn_w_in, ffn_w_out, sgu_w_in, sgu_b_in, sgu_ln_g, sgu_ln_b,
              sgu_w_sp, sgu_b_sp, sgu_w_out, rwkv_mu, rwkv_w_rkv, rwkv_w0, rwkv_w1, rwkv_w2,
              rwkv_a0, rwkv_a1, rwkv_a2, rwkv_v0, rwkv_v1, rwkv_v2, rwkv_g1, rwkv_g2,
              rwkv_k_k, rwkv_k_a, rwkv_r_k, rwkv_gn_g, rwkv_gn_b, rwkv_w_o):
    v_first = None
    for i in range(DEPTH):
        x = layer_norm(DN_ALPHA * x + 0.5 * swiglu(x, ffn_w_in[i, 0], ffn_w_out[i, 0]),
                       ln_g[i, 0], ln_b[i, 0])
        j = i // N_MIXERS
        if i % N_MIXERS == 0:
            mix = sgu_mixer(x, sgu_w_in[j], sgu_b_in[j], sgu_ln_g[j], sgu_ln_b[j],
                            sgu_w_sp[j], sgu_b_sp[j], sgu_w_out[j])
        else:
            vres = None if v_first is None else (rwkv_v0[j - 1], rwkv_v1[j - 1], rwkv_v2[j - 1])
            mix, v_first = rwkv7_mixer(x, v_first, vres, rwkv_mu[j], rwkv_w_rkv[j], rwkv_w0[j],
                                       rwkv_w1[j], rwkv_w2[j], rwkv_a0[j], rwkv_a1[j], rwkv_a2[j],
                                       rwkv_g1[j], rwkv_g2[j], rwkv_k_k[j], rwkv_k_a[j],
                                       rwkv_r_k[j], rwkv_gn_g[j], rwkv_gn_b[j], rwkv_w_o[j])
        x = layer_norm(DN_ALPHA * x + mix, ln_g[i, 1], ln_b[i, 1])
        x = layer_norm(DN_ALPHA * x + 0.5 * swiglu(x, ffn_w_in[i, 1], ffn_w_out[i, 1]),
                       ln_g[i, 2], ln_b[i, 2])
    return x
```

```python
import functools
import math

import jax
import jax.numpy as jnp
from jax import lax
from jax.experimental import pallas as pl
from jax.experimental.pallas import tpu as pltpu

F32 = jnp.float32
BF16 = jnp.bfloat16

LN_EPS = 1e-5
GN_EPS = 64e-5
SGU_CHUNK = 128
SGU_GROUP_DIM = 128
RWKV_HEAD = 64

VMEM_LIMIT_BYTES = 56 * 1024 * 1024

FFN_TM = 512
FFN_FC = 256
SGU_TM = 512
SGU_UC = 512
PROJ_TM = 512
OUT_TM = 512
SCAN_TT = 32


def _dot(a, b):
    return jnp.dot(a, b, preferred_element_type=F32)


def _layer_norm(z, g, b, eps=LN_EPS):
    mu = jnp.mean(z, axis=-1, keepdims=True)
    zc = z - mu
    var = jnp.mean(zc * zc, axis=-1, keepdims=True)
    return zc * lax.rsqrt(var + eps) * g + b


def _const_spec(shape):
    nd = len(shape)
    return pl.BlockSpec(shape, lambda i: (0,) * nd, pipeline_mode=pl.Buffered(1))


def _params(sem):
    return pltpu.CompilerParams(dimension_semantics=(sem,), vmem_limit_bytes=VMEM_LIMIT_BYTES)


def _ffn_body(x_ref, win_ref, wout_ref, g_ref, b_ref, o_ref, act_ref, *, d_ff, fc, alpha):
    x = x_ref[...]
    xb = x.astype(BF16)
    for c in range(d_ff // fc):
        lo = c * fc
        gate = _dot(xb, win_ref[:, lo:lo + fc])
        up = _dot(xb, win_ref[:, d_ff + lo:d_ff + lo + fc])
        act_ref[:, lo:lo + fc] = (gate * jax.nn.sigmoid(gate) * up).astype(BF16)
    o = _dot(act_ref[...], wout_ref[...])
    o_ref[...] = _layer_norm(alpha * x + 0.5 * o, g_ref[...], b_ref[...])


def _ffn(x, w_in, w_out, ln_g, ln_b, alpha, tm=FFN_TM, fc=FFN_FC):
    t, d = x.shape
    d_ff = w_out.shape[0]
    tm = min(tm, t)
    fc = fc if d_ff % fc == 0 else d_ff
    body = functools.partial(_ffn_body, d_ff=d_ff, fc=fc, alpha=alpha)
    return pl.pallas_call(
        body,
        grid=(t // tm,),
        in_specs=[
            pl.BlockSpec((tm, d), lambda i: (i, 0)),
            _const_spec((d, 2 * d_ff)),
            _const_spec((d_ff, d)),
            _const_spec((1, d)),
            _const_spec((1, d)),
        ],
        out_specs=pl.BlockSpec((tm, d), lambda i: (i, 0)),
        out_shape=jax.ShapeDtypeStruct((t, d), F32),
        scratch_shapes=[pltpu.VMEM((tm, d_ff), BF16)],
        compiler_params=_params("parallel"),
        name="ffn",
    )(x, w_in, w_out, ln_g, ln_b)


def _gelu(z):
    return 0.5 * z * (1.0 + lax.erf(z * math.sqrt(0.5)))


def _sgu_body(x_ref, win_ref, bin_ref, lg_ref, lb_ref, wsp_ref, bsp_ref, wout_ref, g_ref, b_ref,
              o_ref, vn_ref, gated_ref, *, d_sgu, uc, alpha):
    tm = x_ref.shape[0]
    x = x_ref[...]
    xb = x.astype(BF16)
    v = _gelu(_dot(xb, win_ref[:, d_sgu:]) + bin_ref[:, d_sgu:])
    vn_ref[...] = _layer_norm(v, lg_ref[...], lb_ref[...]).astype(BF16)
    row = lax.broadcasted_iota(jnp.int32, (SGU_CHUNK, SGU_CHUNK), 0)
    col = lax.broadcasted_iota(jnp.int32, (SGU_CHUNK, SGU_CHUNK), 1)
    causal = col <= row
    groups_per_uc = uc // SGU_GROUP_DIM
    for c in range(d_sgu // uc):
        u = _gelu(_dot(xb, win_ref[:, c * uc:(c + 1) * uc]) + bin_ref[:, c * uc:(c + 1) * uc])
        for gi in range(groups_per_uc):
            g = c * groups_per_uc + gi
            w = jnp.where(causal, wsp_ref[g], 0.0).astype(BF16)
            lanes = slice(g * SGU_GROUP_DIM, (g + 1) * SGU_GROUP_DIM)
            for ci in range(tm // SGU_CHUNK):
                rows = slice(ci * SGU_CHUNK, (ci + 1) * SGU_CHUNK)
                mixed = _dot(w, vn_ref[rows, lanes]) + bsp_ref[g]
                ug = u[rows, gi * SGU_GROUP_DIM:(gi + 1) * SGU_GROUP_DIM]
                gated_ref[rows, lanes] = (ug * mixed).astype(BF16)
    o = _dot(gated_ref[...], wout_ref[...])
    o_ref[...] = _layer_norm(alpha * x + o, g_ref[...], b_ref[...])


def _sgu(x, w_in, b_in, sln_g, sln_b, w_sp, b_sp_bcast, w_out, ln_g, ln_b, alpha, tm=SGU_TM, uc=SGU_UC):
    t, d = x.shape
    d_sgu = w_out.shape[0]
    n_groups = w_sp.shape[0]
    tm = min(tm, t)
    body = functools.partial(_sgu_body, d_sgu=d_sgu, uc=uc, alpha=alpha)
    return pl.pallas_call(
        body,
        grid=(t // tm,),
        in_specs=[
            pl.BlockSpec((tm, d), lambda i: (i, 0)),
            _const_spec((d, 2 * d_sgu)),
            _const_spec((1, 2 * d_sgu)),
            _const_spec((1, d_sgu)),
            _const_spec((1, d_sgu)),
            _const_spec((n_groups, SGU_CHUNK, SGU_CHUNK)),
            _const_spec((n_groups, SGU_CHUNK, SGU_GROUP_DIM)),
            _const_spec((d_sgu, d)),
            _const_spec((1, d)),
            _const_spec((1, d)),
        ],
        out_specs=pl.BlockSpec((tm, d), lambda i: (i, 0)),
        out_shape=jax.ShapeDtypeStruct((t, d), F32),
        scratch_shapes=[pltpu.VMEM((tm, d_sgu), BF16), pltpu.VMEM((tm, d_sgu), BF16)],
        compiler_params=_params("parallel"),
        name="sgu",
    )(x, w_in, b_in, sln_g, sln_b, w_sp, b_sp_bcast, w_out, ln_g, ln_b)


def _softplus(z):
    return jnp.maximum(z, 0.0) + jnp.log1p(jnp.exp(-jnp.abs(z)))


def _proj_body(*refs, tiles_per_seq, has_vres):
    if has_vres:
        (x_ref, xp_ref, mu_ref, wr_ref, wk_ref, wv_ref, w0_ref, w1_ref, w2_ref, a0_ref, a1_ref, a2_ref,
         g1_ref, g2_ref, vf_ref, v0_ref, v1_ref, v2_ref,
         r_out, w_out, k_out, v_out, a_out, g_out) = refs
    else:
        (x_ref, xp_ref, mu_ref, wr_ref, wk_ref, wv_ref, w0_ref, w1_ref, w2_ref, a0_ref, a1_ref, a2_ref,
         g1_ref, g2_ref,
         r_out, w_out, k_out, v_out, a_out, g_out) = refs
    tm = x_ref.shape[0]
    x = x_ref[...]
    seq_start = (pl.program_id(0) % tiles_per_seq) == 0
    prev = jnp.where(seq_start, 0.0, xp_ref[7:8, :])
    first_row = lax.broadcasted_iota(jnp.int32, (tm, 1), 0) == 0
    xx = jnp.where(first_row, prev, pltpu.roll(x, 1, axis=0)) - x

    def mix(n):
        return (x + xx * mu_ref[n:n + 1, :]).astype(BF16)

    r_out[...] = _dot(mix(0), wr_ref[...])
    k_out[...] = _dot(mix(1), wk_ref[...])
    xv = mix(2)
    v = _dot(xv, wv_ref[...])
    if has_vres:
        gate = jax.nn.sigmoid(v0_ref[...] + _dot(_dot(xv, v1_ref[...]).astype(BF16), v2_ref[...]))
        v = v + (vf_ref[...] - v) * gate
    v_out[...] = v
    lw = w0_ref[...] + _dot(jnp.tanh(_dot(mix(3), w1_ref[...])).astype(BF16), w2_ref[...])
    w_out[...] = -_softplus(-lw) - 0.5
    a_out[...] = jax.nn.sigmoid(a0_ref[...] + _dot(_dot(mix(4), a1_ref[...]).astype(BF16), a2_ref[...]))
    g_out[...] = _dot(jax.nn.sigmoid(_dot(mix(5), g1_ref[...])).astype(BF16), g2_ref[...])


def _rwkv_proj(x, seq, mu, w_rkv, w0, w1, w2, a0, a1, a2, g1, g2, vres, tm=PROJ_TM):
    t, d = x.shape
    tm = min(tm, seq)
    has_vres = vres is not None
    tile = pl.BlockSpec((tm, d), lambda i: (i, 0))
    prev_rows = pl.BlockSpec((8, d), lambda i: (jnp.maximum(i * (tm // 8) - 1, 0), 0))
    args = [x, x, mu, w_rkv[0], w_rkv[1], w_rkv[2], w0, w1, w2, a0, a1, a2, g1, g2]
    specs = [tile, prev_rows] + [_const_spec(a.shape) for a in args[2:]]
    if has_vres:
        v_first, v0, v1, v2 = vres
        args += [v_first, v0, v1, v2]
        specs += [tile, _const_spec(v0.shape), _const_spec(v1.shape), _const_spec(v2.shape)]
    body = functools.partial(_proj_body, tiles_per_seq=seq // tm, has_vres=has_vres)
    return pl.pallas_call(
        body,
        grid=(t // tm,),
        in_specs=specs,
        out_specs=[tile] * 6,
        out_shape=[jax.ShapeDtypeStruct((t, d), F32)] * 6,
        compiler_params=_params("parallel"),
        name="rwkv_proj",
    )(*args)


def _scan_body(r_ref, w_ref, k_ref, v_ref, a_ref, kk_ref, ka_ref, rk_ref, gg_ref, gb_ref,
               o_ref, s_ref, av_ref, bv_ref, k2_ref, dec_ref, rw_ref):
    tt, n_head, _ = r_ref.shape

    @pl.when(pl.program_id(0) == 0)
    def _():
        s_ref[...] = jnp.zeros_like(s_ref)

    def step(t, carry):
        r = r_ref[t]
        k = k_ref[t]
        v = v_ref[t]
        a = a_ref[t]
        kk = k * kk_ref[...]
        norm = jnp.sqrt(jnp.sum(kk * kk, axis=0, keepdims=True))
        kk = kk / jnp.maximum(norm, 1e-12)
        k2 = k * (1.0 + (a - 1.0) * ka_ref[...])
        dec = jnp.exp(-jnp.exp(w_ref[t]))
        bv = kk * a
        av_ref[...] = -kk
        bv_ref[...] = bv
        k2_ref[...] = k2
        dec_ref[...] = dec
        rw_ref[...] = r * dec
        rk2 = r * k2
        c_b = jnp.sum(r * bv, axis=0, keepdims=True)
        c_k = jnp.sum(rk2, axis=0, keepdims=True)
        c_bonus = jnp.sum(rk2 * rk_ref[...], axis=0, keepdims=True)

        sa = jnp.zeros_like(v)
        ya = jnp.zeros_like(v)
        for j in range(n_head):
            s = s_ref[j]
            sa = sa + s * av_ref[pl.ds(j, 1), :]
            ya = ya + s * rw_ref[pl.ds(j, 1), :]
        for j in range(n_head):
            s_ref[j] = (s_ref[j] * dec_ref[pl.ds(j, 1), :] + sa * bv_ref[pl.ds(j, 1), :]
                        + v * k2_ref[pl.ds(j, 1), :])
        y = ya + sa * c_b + v * c_k
        mean = jnp.mean(y, axis=0, keepdims=True)
        yc = y - mean
        var = jnp.mean(yc * yc, axis=0, keepdims=True)
        o_ref[t] = yc * lax.rsqrt(var + GN_EPS) * gg_ref[...] + gb_ref[...] + c_bonus * v
        return carry

    lax.fori_loop(0, tt, step, 0)


def _rwkv_scan(r, w, k, v, a, kk_p, ka_p, rk_p, gg_p, gb_p, tt=SCAN_TT):
    s, n_head, n_inst = r.shape
    tt = min(tt, s)
    tile = pl.BlockSpec((tt, n_head, n_inst), lambda i: (i, 0, 0))
    par = _const_spec((n_head, n_inst))
    vm = pltpu.VMEM((n_head, n_inst), F32)
    return pl.pallas_call(
        _scan_body,
        grid=(s // tt,),
        in_specs=[tile] * 5 + [par] * 5,
        out_specs=tile,
        out_shape=jax.ShapeDtypeStruct((s, n_head, n_inst), F32),
        scratch_shapes=[pltpu.VMEM((n_head, n_head, n_inst), F32), vm, vm, vm, vm, vm],
        compiler_params=_params("arbitrary"),
        name="rwkv_scan",
    )(r, w, k, v, a, kk_p, ka_p, rk_p, gg_p, gb_p)


def _out_body(x_ref, y_ref, g_ref, wo_ref, lg_ref, lb_ref, o_ref, *, alpha):
    o = _dot((y_ref[...] * g_ref[...]).astype(BF16), wo_ref[...])
    o_ref[...] = _layer_norm(alpha * x_ref[...] + o, lg_ref[...], lb_ref[...])


def _rwkv_out(x, y, g, w_o, ln_g, ln_b, alpha, tm=OUT_TM):
    t, d = x.shape
    tm = min(tm, t)
    tile = pl.BlockSpec((tm, d), lambda i: (i, 0))
    return pl.pallas_call(
        functools.partial(_out_body, alpha=alpha),
        grid=(t // tm,),
        in_specs=[tile, tile, tile, _const_spec((d, d)), _const_spec((1, d)), _const_spec((1, d))],
        out_specs=tile,
        out_shape=jax.ShapeDtypeStruct((t, d), F32),
        compiler_params=_params("parallel"),
        name="rwkv_out",
    )(x, y, g, w_o, ln_g, ln_b)


def _to_inst(z, bsz, seq, n_heads):
    z = z.reshape(bsz, seq, n_heads, RWKV_HEAD)
    return z.transpose(1, 3, 0, 2).reshape(seq, RWKV_HEAD, bsz * n_heads)


def _from_inst(z, bsz, seq, n_heads):
    z = z.reshape(seq, RWKV_HEAD, bsz, n_heads)
    return z.transpose(2, 0, 3, 1).reshape(bsz * seq, n_heads * RWKV_HEAD)


def _param_inst(p, bsz, n_heads):
    return jnp.tile(p.reshape(n_heads, RWKV_HEAD).T, (1, bsz))


def _row(p):
    return p.reshape(1, -1)


def kernel(x, ln_g, ln_b, ffn_w_in, ffn_w_out, sgu_w_in, sgu_b_in, sgu_ln_g, sgu_ln_b, sgu_w_sp, sgu_b_sp, sgu_w_out, rwkv_mu, rwkv_w_rkv, rwkv_w0, rwkv_w1, rwkv_w2, rwkv_a0, rwkv_a1, rwkv_a2, rwkv_v0, rwkv_v1, rwkv_v2, rwkv_g1, rwkv_g2, rwkv_k_k, rwkv_k_a, rwkv_r_k, rwkv_gn_g, rwkv_gn_b, rwkv_w_o):
    bsz, seq, d = x.shape
    depth = ln_g.shape[0]
    n_heads = d // RWKV_HEAD
    alpha = (2 * depth) ** 0.25
    h = x.reshape(bsz * seq, d)
    v_first = None
    for i in range(depth):
        h = _ffn(h, ffn_w_in[i, 0].astype(BF16), ffn_w_out[i, 0].astype(BF16),
                 _row(ln_g[i, 0]), _row(ln_b[i, 0]), alpha)
        j = i // 2
        if i % 2 == 0:
            b_sp = jnp.broadcast_to(sgu_b_sp[j][:, :, None], sgu_b_sp[j].shape + (SGU_GROUP_DIM,))
            h = _sgu(h, sgu_w_in[j].astype(BF16), _row(sgu_b_in[j]), _row(sgu_ln_g[j]), _row(sgu_ln_b[j]),
                     sgu_w_sp[j], b_sp, sgu_w_out[j].astype(BF16), _row(ln_g[i, 1]), _row(ln_b[i, 1]), alpha)
        else:
            vres = None
            if v_first is not None:
                vres = (v_first, _row(rwkv_v0[j - 1]), rwkv_v1[j - 1].astype(BF16), rwkv_v2[j - 1].astype(BF16))
            r, w, k, v, a, g = _rwkv_proj(
                h, seq, rwkv_mu[j], rwkv_w_rkv[j].astype(BF16), _row(rwkv_w0[j]),
                rwkv_w1[j].astype(BF16), rwkv_w2[j].astype(BF16), _row(rwkv_a0[j]),
                rwkv_a1[j].astype(BF16), rwkv_a2[j].astype(BF16),
                rwkv_g1[j].astype(BF16), rwkv_g2[j].astype(BF16), vres)
            if v_first is None:
                v_first = v
            ti = functools.partial(_to_inst, bsz=bsz, seq=seq, n_heads=n_heads)
            pi = functools.partial(_param_inst, bsz=bsz, n_heads=n_heads)
            y = _rwkv_scan(ti(r), ti(w), ti(k), ti(v), ti(a), pi(rwkv_k_k[j]), pi(rwkv_k_a[j]),
                           pi(rwkv_r_k[j].reshape(-1)), pi(rwkv_gn_g[j]), pi(rwkv_gn_b[j]))
            y = _from_inst(y, bsz, seq, n_heads)
            h = _rwkv_out(h, y, g, rwkv_w_o[j].astype(BF16), _row(ln_g[i, 1]), _row(ln_b[i, 1]), alpha)
        h = _ffn(h, ffn_w_in[i, 1].astype(BF16), ffn_w_out[i, 1].astype(BF16),
                 _row(ln_g[i, 2]), _row(ln_b[i, 2]), alpha)
    return h.reshape(bsz, seq, d)
```

```python
import functools
import math

import jax
import jax.numpy as jnp
from jax import lax
from jax.experimental import pallas as pl
from jax.experimental.pallas import tpu as pltpu

F32 = jnp.float32
BF16 = jnp.bfloat16

LN_EPS = 1e-5
GN_EPS = 64e-5
SGU_CHUNK = 128
SGU_GROUP_DIM = 128
RWKV_HEAD = 64

VMEM_LIMIT_BYTES = 56 * 1024 * 1024

FFN_TM = 512
FFN_FC = 256
SGU_TM = 512
SGU_UC = 512
PROJ_TM = 512
OUT_TM = 512
SCAN_TT = 64


def _dot(a, b):
    return jnp.dot(a, b, preferred_element_type=F32)


def _layer_norm(z, g, b, eps=LN_EPS):
    mu = jnp.mean(z, axis=-1, keepdims=True)
    zc = z - mu
    var = jnp.mean(zc * zc, axis=-1, keepdims=True)
    return zc * lax.rsqrt(var + eps) * g + b


def _const_spec(shape):
    nd = len(shape)
    return pl.BlockSpec(shape, lambda i: (0,) * nd, pipeline_mode=pl.Buffered(1))


def _params(sem):
    return pltpu.CompilerParams(dimension_semantics=(sem,), vmem_limit_bytes=VMEM_LIMIT_BYTES)


def _ffn_body(x_ref, win_ref, wout_ref, g_ref, b_ref, o_ref, act_ref, *, d_ff, fc, alpha):
    x = x_ref[...]
    xb = x.astype(BF16)
    for c in range(d_ff // fc):
        lo = c * fc
        gate = _dot(xb, win_ref[:, lo:lo + fc])
        up = _dot(xb, win_ref[:, d_ff + lo:d_ff + lo + fc])
        act_ref[:, lo:lo + fc] = (gate * jax.nn.sigmoid(gate) * up).astype(BF16)
    o = _dot(act_ref[...], wout_ref[...])
    o_ref[...] = _layer_norm(alpha * x + 0.5 * o, g_ref[...], b_ref[...])


def _ffn(x, w_in, w_out, ln_g, ln_b, alpha, tm=FFN_TM, fc=FFN_FC):
    t, d = x.shape
    d_ff = w_out.shape[0]
    tm = min(tm, t)
    fc = fc if d_ff % fc == 0 else d_ff
    body = functools.partial(_ffn_body, d_ff=d_ff, fc=fc, alpha=alpha)
    return pl.pallas_call(
        body,
        grid=(t // tm,),
        in_specs=[
            pl.BlockSpec((tm, d), lambda i: (i, 0)),
            _const_spec((d, 2 * d_ff)),
            _const_spec((d_ff, d)),
            _const_spec((1, d)),
            _const_spec((1, d)),
        ],
        out_specs=pl.BlockSpec((tm, d), lambda i: (i, 0)),
        out_shape=jax.ShapeDtypeStruct((t, d), F32),
        scratch_shapes=[pltpu.VMEM((tm, d_ff), BF16)],
        compiler_params=_params("parallel"),
        name="ffn",
    )(x, w_in, w_out, ln_g, ln_b)


def _gelu(z):
    return 0.5 * z * (1.0 + lax.erf(z * math.sqrt(0.5)))


def _sgu_body(x_ref, win_ref, bin_ref, lg_ref, lb_ref, wsp_ref, bsp_ref, wout_ref, g_ref, b_ref,
              o_ref, vn_ref, gated_ref, *, d_sgu, uc, alpha):
    tm = x_ref.shape[0]
    x = x_ref[...]
    xb = x.astype(BF16)
    v = _gelu(_dot(xb, win_ref[:, d_sgu:]) + bin_ref[:, d_sgu:])
    vn_ref[...] = _layer_norm(v, lg_ref[...], lb_ref[...]).astype(BF16)
    row = lax.broadcasted_iota(jnp.int32, (SGU_CHUNK, SGU_CHUNK), 0)
    col = lax.broadcasted_iota(jnp.int32, (SGU_CHUNK, SGU_CHUNK), 1)
    causal = col <= row
    groups_per_uc = uc // SGU_GROUP_DIM
    for c in range(d_sgu // uc):
        u = _gelu(_dot(xb, win_ref[:, c * uc:(c + 1) * uc]) + bin_ref[:, c * uc:(c + 1) * uc])
        for gi in range(groups_per_uc):
            g = c * groups_per_uc + gi
            w = jnp.where(causal, wsp_ref[g], 0.0).astype(BF16)
            lanes = slice(g * SGU_GROUP_DIM, (g + 1) * SGU_GROUP_DIM)
            for ci in range(tm // SGU_CHUNK):
                rows = slice(ci * SGU_CHUNK, (ci + 1) * SGU_CHUNK)
                mixed = _dot(w, vn_ref[rows, lanes]) + bsp_ref[g]
                ug = u[rows, gi * SGU_GROUP_DIM:(gi + 1) * SGU_GROUP_DIM]
                gated_ref[rows, lanes] = (ug * mixed).astype(BF16)
    o = _dot(gated_ref[...], wout_ref[...])
    o_ref[...] = _layer_norm(alpha * x + o, g_ref[...], b_ref[...])


def _sgu(x, w_in, b_in, sln_g, sln_b, w_sp, b_sp_bcast, w_out, ln_g, ln_b, alpha, tm=SGU_TM, uc=SGU_UC):
    t, d = x.shape
    d_sgu = w_out.shape[0]
    n_groups = w_sp.shape[0]
    tm = min(tm, t)
    body = functools.partial(_sgu_body, d_sgu=d_sgu, uc=uc, alpha=alpha)
    return pl.pallas_call(
        body,
        grid=(t // tm,),
        in_specs=[
            pl.BlockSpec((tm, d), lambda i: (i, 0)),
            _const_spec((d, 2 * d_sgu)),
            _const_spec((1, 2 * d_sgu)),
            _const_spec((1, d_sgu)),
            _const_spec((1, d_sgu)),
            _const_spec((n_groups, SGU_CHUNK, SGU_CHUNK)),
            _const_spec((n_groups, SGU_CHUNK, SGU_GROUP_DIM)),
            _const_spec((d_sgu, d)),
            _const_spec((1, d)),
            _const_spec((1, d)),
        ],
        out_specs=pl.BlockSpec((tm, d), lambda i: (i, 0)),
        out_shape=jax.ShapeDtypeStruct((t, d), F32),
        scratch_shapes=[pltpu.VMEM((tm, d_sgu), BF16), pltpu.VMEM((tm, d_sgu), BF16)],
        compiler_params=_params("parallel"),
        name="sgu",
    )(x, w_in, b_in, sln_g, sln_b, w_sp, b_sp_bcast, w_out, ln_g, ln_b)


def _softplus(z):
    return jnp.maximum(z, 0.0) + jnp.log1p(jnp.exp(-jnp.abs(z)))


def _proj_body(*refs, tiles_per_seq, has_vres):
    if has_vres:
        (x_ref, xp_ref, mu_ref, wr_ref, wk_ref, wv_ref, w0_ref, w1_ref, w2_ref, a0_ref, a1_ref, a2_ref,
         g1_ref, g2_ref, vf_ref, v0_ref, v1_ref, v2_ref,
         r_out, w_out, k_out, v_out, a_out, g_out) = refs
    else:
        (x_ref, xp_ref, mu_ref, wr_ref, wk_ref, wv_ref, w0_ref, w1_ref, w2_ref, a0_ref, a1_ref, a2_ref,
         g1_ref, g2_ref,
         r_out, w_out, k_out, v_out, a_out, g_out) = refs
    tm = x_ref.shape[0]
    x = x_ref[...]
    seq_start = (pl.program_id(0) % tiles_per_seq) == 0
    prev = jnp.where(seq_start, 0.0, xp_ref[7:8, :])
    first_row = lax.broadcasted_iota(jnp.int32, (tm, 1), 0) == 0
    xx = jnp.where(first_row, prev, pltpu.roll(x, 1, axis=0)) - x

    def mix(n):
        return (x + xx * mu_ref[n:n + 1, :]).astype(BF16)

    r_out[...] = _dot(mix(0), wr_ref[...])
    k_out[...] = _dot(mix(1), wk_ref[...])
    xv = mix(2)
    v = _dot(xv, wv_ref[...])
    if has_vres:
        gate = jax.nn.sigmoid(v0_ref[...] + _dot(_dot(xv, v1_ref[...]).astype(BF16), v2_ref[...]))
        v = v + (vf_ref[...] - v) * gate
    v_out[...] = v
    lw = w0_ref[...] + _dot(jnp.tanh(_dot(mix(3), w1_ref[...])).astype(BF16), w2_ref[...])
    w_out[...] = -_softplus(-lw) - 0.5
    a_out[...] = jax.nn.sigmoid(a0_ref[...] + _dot(_dot(mix(4), a1_ref[...]).astype(BF16), a2_ref[...]))
    g_out[...] = _dot(jax.nn.sigmoid(_dot(mix(5), g1_ref[...])).astype(BF16), g2_ref[...])


def _rwkv_proj(x, seq, mu, w_rkv, w0, w1, w2, a0, a1, a2, g1, g2, vres, tm=PROJ_TM):
    t, d = x.shape
    tm = min(tm, seq)
    has_vres = vres is not None
    tile = pl.BlockSpec((tm, d), lambda i: (i, 0))
    prev_rows = pl.BlockSpec((8, d), lambda i: (jnp.maximum(i * (tm // 8) - 1, 0), 0))
    args = [x, x, mu, w_rkv[0], w_rkv[1], w_rkv[2], w0, w1, w2, a0, a1, a2, g1, g2]
    specs = [tile, prev_rows] + [_const_spec(a.shape) for a in args[2:]]
    if has_vres:
        v_first, v0, v1, v2 = vres
        args += [v_first, v0, v1, v2]
        specs += [tile, _const_spec(v0.shape), _const_spec(v1.shape), _const_spec(v2.shape)]
    body = functools.partial(_proj_body, tiles_per_seq=seq // tm, has_vres=has_vres)
    return pl.pallas_call(
        body,
        grid=(t // tm,),
        in_specs=specs,
        out_specs=[tile] * 6,
        out_shape=[jax.ShapeDtypeStruct((t, d), F32)] * 6,
        compiler_params=_params("parallel"),
        name="rwkv_proj",
    )(*args)


_AV, _RW, _DEC, _BV, _K2, _V = range(6)
_C_B, _C_K, _C_BONUS = range(3)


def _scan_body(r_ref, w_ref, k_ref, v_ref, a_ref, kk_ref, ka_ref, rk_ref, gg_ref, gb_ref,
               o_ref, s_ref, ops_ref, crow_ref):
    tt = r_ref.shape[1]
    n = RWKV_HEAD
    half = 64
    lane = lax.broadcasted_iota(jnp.int32, (n, 2 * half), 1)
    low = lane < half

    @pl.when(pl.program_id(0) == 0)
    def _():
        s_ref[...] = jnp.zeros_like(s_ref)

    def load_pair(x_ref, t):
        m0 = x_ref[:, pl.ds(t, 1)].reshape(half, 2 * n)
        m1 = x_ref[:, pl.ds(t + 1, 1)].reshape(half, 2 * n)
        tr = jnp.concatenate([m0, m1], axis=0).T
        top, bot = tr[:n], tr[n:]
        return (jnp.where(low, top, pltpu.roll(bot, half, axis=1)),
                jnp.where(low, pltpu.roll(top, half, axis=1), bot))

    def prep(t, slot):
        rs, ws, ks, vs, gs = (load_pair(x, t) for x in (r_ref, w_ref, k_ref, v_ref, a_ref))
        for st in range(2):
            r, k, v, a = rs[st], ks[st], vs[st], gs[st]
            kk = k * kk_ref[...]
            norm = jnp.sqrt(jnp.sum(kk * kk, axis=0, keepdims=True))
            kk = kk / jnp.maximum(norm, 1e-12)
            k2 = k * (1.0 + (a - 1.0) * ka_ref[...])
            dec = jnp.exp(-jnp.exp(ws[st]))
            bv = kk * a
            rk2 = r * k2
            ops_ref[slot, st, _AV] = -kk
            ops_ref[slot, st, _RW] = r * dec
            ops_ref[slot, st, _DEC] = dec
            ops_ref[slot, st, _BV] = bv
            ops_ref[slot, st, _K2] = k2
            ops_ref[slot, st, _V] = v
            crow_ref[slot, st, _C_B:_C_B + 1] = jnp.sum(r * bv, axis=0, keepdims=True)
            crow_ref[slot, st, _C_K:_C_K + 1] = jnp.sum(rk2, axis=0, keepdims=True)
            crow_ref[slot, st, _C_BONUS:_C_BONUS + 1] = jnp.sum(rk2 * rk_ref[...], axis=0, keepdims=True)

    def row(slot, st, which, j):
        return ops_ref[slot, st, which, pl.ds(j, 1), :]

    def reduce_only(slot, st):
        sa = jnp.zeros((n, 2 * half), F32)
        ya = jnp.zeros((n, 2 * half), F32)
        for j in range(n):
            s = s_ref[j]
            sa = sa + s * row(slot, st, _AV, j)
            ya = ya + s * row(slot, st, _RW, j)
        return sa, ya

    def advance(sa, ya, upd, nxt):
        us, ut = upd
        ns, nt = nxt
        v = ops_ref[us, ut, _V]
        y = (ya + sa * crow_ref[us, ut, _C_B:_C_B + 1] + v * crow_ref[us, ut, _C_K:_C_K + 1])
        mean = jnp.mean(y, axis=0, keepdims=True)
        yc = y - mean
        var = jnp.mean(yc * yc, axis=0, keepdims=True)
        out = (yc * lax.rsqrt(var + GN_EPS) * gg_ref[...] + gb_ref[...]
               + crow_ref[us, ut, _C_BONUS:_C_BONUS + 1] * v)
        sa_n = jnp.zeros_like(sa)
        ya_n = jnp.zeros_like(ya)
        for j in range(n):
            s = s_ref[j] * row(us, ut, _DEC, j) + sa * row(us, ut, _BV, j) + v * row(us, ut, _K2, j)
            s_ref[j] = s
            sa_n = sa_n + s * row(ns, nt, _AV, j)
            ya_n = ya_n + s * row(ns, nt, _RW, j)
        return out, sa_n, ya_n

    def store_pair(t, y0, y1):
        top = jnp.where(low, y0, pltpu.roll(y1, half, axis=1))
        bot = jnp.where(low, pltpu.roll(y0, half, axis=1), y1)
        back = jnp.concatenate([top, bot], axis=0).T
        o_ref[:, pl.ds(t, 1)] = back[:half].reshape(o_ref.shape[0], 1, 8, 2 * n)
        o_ref[:, pl.ds(t + 1, 1)] = back[half:].reshape(o_ref.shape[0], 1, 8, 2 * n)

    prep(0, 0)
    sa0, ya0 = reduce_only(0, 0)

    def quad(q, carry):
        sa, ya = carry
        t = q * 4
        prep(t + 2, 1)
        y0, sa, ya = advance(sa, ya, (0, 0), (0, 1))
        y1, sa, ya = advance(sa, ya, (0, 1), (1, 0))
        store_pair(t, y0, y1)
        prep(jnp.minimum(t + 4, tt - 2), 0)
        y2, sa, ya = advance(sa, ya, (1, 0), (1, 1))
        y3, sa, ya = advance(sa, ya, (1, 1), (0, 0))
        store_pair(t + 2, y2, y3)
        return sa, ya

    lax.fori_loop(0, tt // 4, quad, (sa0, ya0))


def _rwkv_scan(r, w, k, v, a, kk_p, ka_p, rk_p, gg_p, gb_p, tt=SCAN_TT):
    bsz, s, n_pairs, lanes = r.shape
    tt = min(tt, s)
    n = RWKV_HEAD
    assert tt % 4 == 0 and bsz * n_pairs == 64 and lanes == 2 * n
    tile = pl.BlockSpec((bsz, tt, n_pairs, lanes), lambda i: (0, i, 0, 0))
    par = _const_spec((n, lanes))
    return pl.pallas_call(
        _scan_body,
        grid=(s // tt,),
        in_specs=[tile] * 5 + [par] * 5,
        out_specs=tile,
        out_shape=jax.ShapeDtypeStruct(r.shape, F32),
        scratch_shapes=[pltpu.VMEM((n, n, lanes), F32),
                        pltpu.VMEM((2, 2, 6, n, lanes), F32),
                        pltpu.VMEM((2, 2, 8, lanes), F32)],
        compiler_params=_params("arbitrary"),
        name="rwkv_scan",
    )(r, w, k, v, a, kk_p, ka_p, rk_p, gg_p, gb_p)


def _out_body(x_ref, y_ref, g_ref, wo_ref, lg_ref, lb_ref, o_ref, *, alpha):
    o = _dot((y_ref[...] * g_ref[...]).astype(BF16), wo_ref[...])
    o_ref[...] = _layer_norm(alpha * x_ref[...] + o, lg_ref[...], lb_ref[...])


def _rwkv_out(x, y, g, w_o, ln_g, ln_b, alpha, tm=OUT_TM):
    t, d = x.shape
    tm = min(tm, t)
    tile = pl.BlockSpec((tm, d), lambda i: (i, 0))
    return pl.pallas_call(
        functools.partial(_out_body, alpha=alpha),
        grid=(t // tm,),
        in_specs=[tile, tile, tile, _const_spec((d, d)), _const_spec((1, d)), _const_spec((1, d))],
        out_specs=tile,
        out_shape=jax.ShapeDtypeStruct((t, d), F32),
        compiler_params=_params("parallel"),
        name="rwkv_out",
    )(x, y, g, w_o, ln_g, ln_b)


def _param_inst(p, bsz, n_heads):
    n_pairs = n_heads // 2
    t = p.reshape(n_pairs, 2, RWKV_HEAD).transpose(2, 1, 0)
    t = jnp.broadcast_to(t[:, :, None, :], (RWKV_HEAD, 2, bsz, n_pairs))
    return t.reshape(RWKV_HEAD, 2 * bsz * n_pairs)


def _row(p):
    return p.reshape(1, -1)


def kernel(x, ln_g, ln_b, ffn_w_in, ffn_w_out, sgu_w_in, sgu_b_in, sgu_ln_g, sgu_ln_b, sgu_w_sp, sgu_b_sp, sgu_w_out, rwkv_mu, rwkv_w_rkv, rwkv_w0, rwkv_w1, rwkv_w2, rwkv_a0, rwkv_a1, rwkv_a2, rwkv_v0, rwkv_v1, rwkv_v2, rwkv_g1, rwkv_g2, rwkv_k_k, rwkv_k_a, rwkv_r_k, rwkv_gn_g, rwkv_gn_b, rwkv_w_o):
    bsz, seq, d = x.shape
    depth = ln_g.shape[0]
    n_heads = d // RWKV_HEAD
    alpha = (2 * depth) ** 0.25
    h = x.reshape(bsz * seq, d)
    v_first = None
    for i in range(depth):
        h = _ffn(h, ffn_w_in[i, 0].astype(BF16), ffn_w_out[i, 0].astype(BF16),
                 _row(ln_g[i, 0]), _row(ln_b[i, 0]), alpha)
        j = i // 2
        if i % 2 == 0:
            b_sp = jnp.broadcast_to(sgu_b_sp[j][:, :, None], sgu_b_sp[j].shape + (SGU_GROUP_DIM,))
            h = _sgu(h, sgu_w_in[j].astype(BF16), _row(sgu_b_in[j]), _row(sgu_ln_g[j]), _row(sgu_ln_b[j]),
                     sgu_w_sp[j], b_sp, sgu_w_out[j].astype(BF16), _row(ln_g[i, 1]), _row(ln_b[i, 1]), alpha)
        else:
            vres = None
            if v_first is not None:
                vres = (v_first, _row(rwkv_v0[j - 1]), rwkv_v1[j - 1].astype(BF16), rwkv_v2[j - 1].astype(BF16))
            r, w, k, v, a, g = _rwkv_proj(
                h, seq, rwkv_mu[j], rwkv_w_rkv[j].astype(BF16), _row(rwkv_w0[j]),
                rwkv_w1[j].astype(BF16), rwkv_w2[j].astype(BF16), _row(rwkv_a0[j]),
                rwkv_a1[j].astype(BF16), rwkv_a2[j].astype(BF16),
                rwkv_g1[j].astype(BF16), rwkv_g2[j].astype(BF16), vres)
            if v_first is None:
                v_first = v
            def ti(z):
                return z.reshape(bsz, seq, n_heads // 2, 2 * RWKV_HEAD)
            pi = functools.partial(_param_inst, bsz=bsz, n_heads=n_heads)
            y = _rwkv_scan(ti(r), ti(w), ti(k), ti(v), ti(a), pi(rwkv_k_k[j]), pi(rwkv_k_a[j]),
                           pi(rwkv_r_k[j].reshape(-1)), pi(rwkv_gn_g[j]), pi(rwkv_gn_b[j]))
            y = y.reshape(bsz * seq, d)
            h = _rwkv_out(h, y, g, rwkv_w_o[j].astype(BF16), _row(ln_g[i, 1]), _row(ln_b[i, 1]), alpha)
        h = _ffn(h, ffn_w_in[i, 1].astype(BF16), ffn_w_out[i, 1].astype(BF16),
                 _row(ln_g[i, 2]), _row(ln_b[i, 2]), alpha)
    return h.reshape(bsz, seq, d)
```

```python
import functools
import math

import jax
import jax.numpy as jnp
from jax import lax
from jax.experimental import pallas as pl
from jax.experimental.pallas import tpu as pltpu

F32 = jnp.float32
BF16 = jnp.bfloat16

LN_EPS = 1e-5
GN_EPS = 64e-5
SGU_CHUNK = 128
SGU_GROUP_DIM = 128
RWKV_HEAD = 64

VMEM_LIMIT_BYTES = 56 * 1024 * 1024

FFN_TM = 512
FFN_FC = 256
SGU_TM = 512
SGU_UC = 512
PROJ_TM = 512
OUT_TM = 512
SCAN_TT = 64


def _dot(a, b):
    return jnp.dot(a, b, preferred_element_type=F32)


def _layer_norm(z, g, b, eps=LN_EPS):
    mu = jnp.mean(z, axis=-1, keepdims=True)
    zc = z - mu
    var = jnp.mean(zc * zc, axis=-1, keepdims=True)
    return zc * lax.rsqrt(var + eps) * g + b


def _const_spec(shape):
    nd = len(shape)
    return pl.BlockSpec(shape, lambda i: (0,) * nd, pipeline_mode=pl.Buffered(1))


def _params(sem):
    return pltpu.CompilerParams(dimension_semantics=(sem,), vmem_limit_bytes=VMEM_LIMIT_BYTES)


def _ffn_body(x_ref, win_ref, wout_ref, g_ref, b_ref, o_ref, act_ref, *, d_ff, fc, alpha):
    x = x_ref[...]
    xb = x.astype(BF16)
    for c in range(d_ff // fc):
        lo = c * fc
        gate = _dot(xb, win_ref[:, lo:lo + fc])
        up = _dot(xb, win_ref[:, d_ff + lo:d_ff + lo + fc])
        act_ref[:, lo:lo + fc] = (gate * jax.nn.sigmoid(gate) * up).astype(BF16)
    o = _dot(act_ref[...], wout_ref[...])
    o_ref[...] = _layer_norm(alpha * x + 0.5 * o, g_ref[...], b_ref[...])


def _ffn(x, w_in, w_out, ln_g, ln_b, alpha, tm=FFN_TM, fc=FFN_FC):
    t, d = x.shape
    d_ff = w_out.shape[0]
    tm = min(tm, t)
    fc = fc if d_ff % fc == 0 else d_ff
    body = functools.partial(_ffn_body, d_ff=d_ff, fc=fc, alpha=alpha)
    return pl.pallas_call(
        body,
        grid=(t // tm,),
        in_specs=[
            pl.BlockSpec((tm, d), lambda i: (i, 0)),
            _const_spec((d, 2 * d_ff)),
            _const_spec((d_ff, d)),
            _const_spec((1, d)),
            _const_spec((1, d)),
        ],
        out_specs=pl.BlockSpec((tm, d), lambda i: (i, 0)),
        out_shape=jax.ShapeDtypeStruct((t, d), F32),
        scratch_shapes=[pltpu.VMEM((tm, d_ff), BF16)],
        compiler_params=_params("parallel"),
        name="ffn",
    )(x, w_in, w_out, ln_g, ln_b)


def _gelu(z):
    return 0.5 * z * (1.0 + lax.erf(z * math.sqrt(0.5)))


def _sgu_body(x_ref, win_ref, bin_ref, lg_ref, lb_ref, wsp_ref, bsp_ref, wout_ref, g_ref, b_ref,
              o_ref, vn_ref, gated_ref, *, d_sgu, uc, alpha):
    tm = x_ref.shape[0]
    x = x_ref[...]
    xb = x.astype(BF16)
    v = _gelu(_dot(xb, win_ref[:, d_sgu:]) + bin_ref[:, d_sgu:])
    vn_ref[...] = _layer_norm(v, lg_ref[...], lb_ref[...]).astype(BF16)
    row = lax.broadcasted_iota(jnp.int32, (SGU_CHUNK, SGU_CHUNK), 0)
    col = lax.broadcasted_iota(jnp.int32, (SGU_CHUNK, SGU_CHUNK), 1)
    causal = col <= row
    groups_per_uc = uc // SGU_GROUP_DIM
    for c in range(d_sgu // uc):
        u = _gelu(_dot(xb, win_ref[:, c * uc:(c + 1) * uc]) + bin_ref[:, c * uc:(c + 1) * uc])
        for gi in range(groups_per_uc):
            g = c * groups_per_uc + gi
            w = jnp.where(causal, wsp_ref[g], 0.0).astype(BF16)
            lanes = slice(g * SGU_GROUP_DIM, (g + 1) * SGU_GROUP_DIM)
            for ci in range(tm // SGU_CHUNK):
                rows = slice(ci * SGU_CHUNK, (ci + 1) * SGU_CHUNK)
                mixed = _dot(w, vn_ref[rows, lanes]) + bsp_ref[g]
                ug = u[rows, gi * SGU_GROUP_DIM:(gi + 1) * SGU_GROUP_DIM]
                gated_ref[rows, lanes] = (ug * mixed).astype(BF16)
    o = _dot(gated_ref[...], wout_ref[...])
    o_ref[...] = _layer_norm(alpha * x + o, g_ref[...], b_ref[...])


def _sgu(x, w_in, b_in, sln_g, sln_b, w_sp, b_sp_bcast, w_out, ln_g, ln_b, alpha, tm=SGU_TM, uc=SGU_UC):
    t, d = x.shape
    d_sgu = w_out.shape[0]
    n_groups = w_sp.shape[0]
    tm = min(tm, t)
    body = functools.partial(_sgu_body, d_sgu=d_sgu, uc=uc, alpha=alpha)
    return pl.pallas_call(
        body,
        grid=(t // tm,),
        in_specs=[
            pl.BlockSpec((tm, d), lambda i: (i, 0)),
            _const_spec((d, 2 * d_sgu)),
            _const_spec((1, 2 * d_sgu)),
            _const_spec((1, d_sgu)),
            _const_spec((1, d_sgu)),
            _const_spec((n_groups, SGU_CHUNK, SGU_CHUNK)),
            _const_spec((n_groups, SGU_CHUNK, SGU_GROUP_DIM)),
            _const_spec((d_sgu, d)),
            _const_spec((1, d)),
            _const_spec((1, d)),
        ],
        out_specs=pl.BlockSpec((tm, d), lambda i: (i, 0)),
        out_shape=jax.ShapeDtypeStruct((t, d), F32),
        scratch_shapes=[pltpu.VMEM((tm, d_sgu), BF16), pltpu.VMEM((tm, d_sgu), BF16)],
        compiler_params=_params("parallel"),
        name="sgu",
    )(x, w_in, b_in, sln_g, sln_b, w_sp, b_sp_bcast, w_out, ln_g, ln_b)


def _softplus(z):
    return jnp.maximum(z, 0.0) + jnp.log1p(jnp.exp(-jnp.abs(z)))


def _proj_body(*refs, has_vres):
    if has_vres:
        (x_ref, xp_ref, mu_ref, wr_ref, wk_ref, wv_ref, w0_ref, w1_ref, w2_ref, a0_ref, a1_ref, a2_ref,
         g1_ref, g2_ref, vf_ref, v0_ref, v1_ref, v2_ref,
         r_out, w_out, k_out, v_out, a_out, g_out) = refs
    else:
        (x_ref, xp_ref, mu_ref, wr_ref, wk_ref, wv_ref, w0_ref, w1_ref, w2_ref, a0_ref, a1_ref, a2_ref,
         g1_ref, g2_ref,
         r_out, w_out, k_out, v_out, a_out, g_out) = refs
    bsz = xp_ref.shape[0]
    x = x_ref[...]
    prev = jnp.where(pl.program_id(0) == 0, 0.0, xp_ref[...])
    xx = jnp.concatenate([prev, x[:-bsz]], axis=0) - x

    def mix(n):
        return (x + xx * mu_ref[n:n + 1, :]).astype(BF16)

    r_out[...] = _dot(mix(0), wr_ref[...])
    k_out[...] = _dot(mix(1), wk_ref[...])
    xv = mix(2)
    v = _dot(xv, wv_ref[...])
    if has_vres:
        gate = jax.nn.sigmoid(v0_ref[...] + _dot(_dot(xv, v1_ref[...]).astype(BF16), v2_ref[...]))
        v = v + (vf_ref[...] - v) * gate
    v_out[...] = v
    lw = w0_ref[...] + _dot(jnp.tanh(_dot(mix(3), w1_ref[...])).astype(BF16), w2_ref[...])
    w_out[...] = -_softplus(-lw) - 0.5
    a_out[...] = jax.nn.sigmoid(a0_ref[...] + _dot(_dot(mix(4), a1_ref[...]).astype(BF16), a2_ref[...]))
    g_out[...] = _dot(jax.nn.sigmoid(_dot(mix(5), g1_ref[...])).astype(BF16), g2_ref[...])


def _rwkv_proj(x, bsz, mu, w_rkv, w0, w1, w2, a0, a1, a2, g1, g2, vres, tm=PROJ_TM):
    t, d = x.shape
    tm = min(tm, t)
    assert tm % bsz == 0 and bsz % 8 == 0
    has_vres = vres is not None
    tile = pl.BlockSpec((tm, d), lambda i: (i, 0))
    prev_rows = pl.BlockSpec((bsz, d), lambda i: (jnp.maximum(i * (tm // bsz) - 1, 0), 0))
    args = [x, x, mu, w_rkv[0], w_rkv[1], w_rkv[2], w0, w1, w2, a0, a1, a2, g1, g2]
    specs = [tile, prev_rows] + [_const_spec(a.shape) for a in args[2:]]
    if has_vres:
        v_first, v0, v1, v2 = vres
        args += [v_first, v0, v1, v2]
        specs += [tile, _const_spec(v0.shape), _const_spec(v1.shape), _const_spec(v2.shape)]
    body = functools.partial(_proj_body, has_vres=has_vres)
    return pl.pallas_call(
        body,
        grid=(t // tm,),
        in_specs=specs,
        out_specs=[tile] * 6,
        out_shape=[jax.ShapeDtypeStruct((t, d), F32)] * 6,
        compiler_params=_params("parallel"),
        name="rwkv_proj",
    )(*args)


_AV, _RW, _DEC, _BV, _K2, _V = range(6)
_C_B, _C_K, _C_BONUS = range(3)


def _scan_body(r_ref, w_ref, k_ref, v_ref, a_ref, kk_ref, ka_ref, rk_ref, gg_ref, gb_ref,
               o_ref, s_ref, ops_ref, crow_ref):
    n = RWKV_HEAD
    half = 64
    n_pairs = r_ref.shape[1] // (2 * n)
    bsz = half // n_pairs
    tt = r_ref.shape[0] // bsz
    lane = lax.broadcasted_iota(jnp.int32, (n, 2 * half), 1)
    low = lane < half

    @pl.when(pl.program_id(0) == 0)
    def _():
        s_ref[...] = jnp.zeros_like(s_ref)

    def load_step(x_ref, t):
        rows = x_ref[pl.ds(pl.multiple_of(t * bsz, bsz), bsz), :]
        return jnp.concatenate([rows[:, hp * 2 * n:(hp + 1) * 2 * n] for hp in range(n_pairs)], axis=0)

    def load_pair(x_ref, t):
        tr = jnp.concatenate([load_step(x_ref, t), load_step(x_ref, t + 1)], axis=0).T
        top, bot = tr[:n], tr[n:]
        return (jnp.where(low, top, pltpu.roll(bot, half, axis=1)),
                jnp.where(low, pltpu.roll(top, half, axis=1), bot))

    def prep(t, slot):
        rs, ws, ks, vs, gs = (load_pair(x, t) for x in (r_ref, w_ref, k_ref, v_ref, a_ref))
        for st in range(2):
            r, k, v, a = rs[st], ks[st], vs[st], gs[st]
            kk = k * kk_ref[...]
            norm = jnp.sqrt(jnp.sum(kk * kk, axis=0, keepdims=True))
            kk = kk / jnp.maximum(norm, 1e-12)
            k2 = k * (1.0 + (a - 1.0) * ka_ref[...])
            dec = jnp.exp(-jnp.exp(ws[st]))
            bv = kk * a
            rk2 = r * k2
            ops_ref[slot, st, _AV] = -kk
            ops_ref[slot, st, _RW] = r * dec
            ops_ref[slot, st, _DEC] = dec
            ops_ref[slot, st, _BV] = bv
            ops_ref[slot, st, _K2] = k2
            ops_ref[slot, st, _V] = v
            crow_ref[slot, st, _C_B:_C_B + 1] = jnp.sum(r * bv, axis=0, keepdims=True)
            crow_ref[slot, st, _C_K:_C_K + 1] = jnp.sum(rk2, axis=0, keepdims=True)
            crow_ref[slot, st, _C_BONUS:_C_BONUS + 1] = jnp.sum(rk2 * rk_ref[...], axis=0, keepdims=True)

    def row(slot, st, which, j):
        return ops_ref[slot, st, which, pl.ds(j, 1), :]

    def reduce_only(slot, st):
        sa = jnp.zeros((n, 2 * half), F32)
        ya = jnp.zeros((n, 2 * half), F32)
        for j in range(n):
            s = s_ref[j]
            sa = sa + s * row(slot, st, _AV, j)
            ya = ya + s * row(slot, st, _RW, j)
        return sa, ya

    def advance(sa, ya, upd, nxt):
        us, ut = upd
        ns, nt = nxt
        v = ops_ref[us, ut, _V]
        y = (ya + sa * crow_ref[us, ut, _C_B:_C_B + 1] + v * crow_ref[us, ut, _C_K:_C_K + 1])
        mean = jnp.mean(y, axis=0, keepdims=True)
        yc = y - mean
        var = jnp.mean(yc * yc, axis=0, keepdims=True)
        out = (yc * lax.rsqrt(var + GN_EPS) * gg_ref[...] + gb_ref[...]
               + crow_ref[us, ut, _C_BONUS:_C_BONUS + 1] * v)
        sa_n = jnp.zeros_like(sa)
        ya_n = jnp.zeros_like(ya)
        for j in range(n):
            s = s_ref[j] * row(us, ut, _DEC, j) + sa * row(us, ut, _BV, j) + v * row(us, ut, _K2, j)
            s_ref[j] = s
            sa_n = sa_n + s * row(ns, nt, _AV, j)
            ya_n = ya_n + s * row(ns, nt, _RW, j)
        return out, sa_n, ya_n

    def store_pair(t, y0, y1):
        top = jnp.where(low, y0, pltpu.roll(y1, half, axis=1))
        bot = jnp.where(low, pltpu.roll(y0, half, axis=1), y1)
        back = jnp.concatenate([top, bot], axis=0).T
        for st in range(2):
            rows = jnp.concatenate(
                [back[st * half + hp * bsz:st * half + (hp + 1) * bsz] for hp in range(n_pairs)], axis=1)
            o_ref[pl.ds(pl.multiple_of((t + st) * bsz, bsz), bsz), :] = rows

    prep(0, 0)
    sa0, ya0 = reduce_only(0, 0)

    def quad(q, carry):
        sa, ya = carry
        t = q * 4
        prep(t + 2, 1)
        y0, sa, ya = advance(sa, ya, (0, 0), (0, 1))
        y1, sa, ya = advance(sa, ya, (0, 1), (1, 0))
        store_pair(t, y0, y1)
        prep(jnp.minimum(t + 4, tt - 2), 0)
        y2, sa, ya = advance(sa, ya, (1, 0), (1, 1))
        y3, sa, ya = advance(sa, ya, (1, 1), (0, 0))
        store_pair(t + 2, y2, y3)
        return sa, ya

    lax.fori_loop(0, tt // 4, quad, (sa0, ya0))


def _rwkv_scan(r, w, k, v, a, bsz, kk_p, ka_p, rk_p, gg_p, gb_p, tt=SCAN_TT):
    rows, d = r.shape
    s = rows // bsz
    tt = min(tt, s)
    n = RWKV_HEAD
    lanes = 2 * n
    assert tt % 4 == 0 and bsz % 8 == 0 and bsz * (d // lanes) == 64
    tile = pl.BlockSpec((tt * bsz, d), lambda i: (i, 0))
    par = _const_spec((n, lanes))
    return pl.pallas_call(
        _scan_body,
        grid=(s // tt,),
        in_specs=[tile] * 5 + [par] * 5,
        out_specs=tile,
        out_shape=jax.ShapeDtypeStruct(r.shape, F32),
        scratch_shapes=[pltpu.VMEM((n, n, lanes), F32),
                        pltpu.VMEM((2, 2, 6, n, lanes), F32),
                        pltpu.VMEM((2, 2, 8, lanes), F32)],
        compiler_params=_params("arbitrary"),
        name="rwkv_scan",
    )(r, w, k, v, a, kk_p, ka_p, rk_p, gg_p, gb_p)


def _out_body(x_ref, y_ref, g_ref, wo_ref, lg_ref, lb_ref, o_ref, *, alpha):
    o = _dot((y_ref[...] * g_ref[...]).astype(BF16), wo_ref[...])
    o_ref[...] = _layer_norm(alpha * x_ref[...] + o, lg_ref[...], lb_ref[...])


def _rwkv_out(x, y, g, w_o, ln_g, ln_b, alpha, tm=OUT_TM):
    t, d = x.shape
    tm = min(tm, t)
    tile = pl.BlockSpec((tm, d), lambda i: (i, 0))
    return pl.pallas_call(
        functools.partial(_out_body, alpha=alpha),
        grid=(t // tm,),
        in_specs=[tile, tile, tile, _const_spec((d, d)), _const_spec((1, d)), _const_spec((1, d))],
        out_specs=tile,
        out_shape=jax.ShapeDtypeStruct((t, d), F32),
        compiler_params=_params("parallel"),
        name="rwkv_out",
    )(x, y, g, w_o, ln_g, ln_b)


def _param_inst(p, bsz, n_heads):
    n_pairs = n_heads // 2
    t = p.reshape(n_pairs, 2, RWKV_HEAD).transpose(2, 1, 0)
    t = jnp.broadcast_to(t[:, :, :, None], (RWKV_HEAD, 2, n_pairs, bsz))
    return t.reshape(RWKV_HEAD, 2 * n_pairs * bsz)


def _swap_token_order(h, rows_major, rows_minor):
    d = h.shape[-1]
    return h.reshape(rows_major, rows_minor, d).transpose(1, 0, 2).reshape(rows_major * rows_minor, d)


def _row(p):
    return p.reshape(1, -1)


def kernel(x, ln_g, ln_b, ffn_w_in, ffn_w_out, sgu_w_in, sgu_b_in, sgu_ln_g, sgu_ln_b, sgu_w_sp, sgu_b_sp, sgu_w_out, rwkv_mu, rwkv_w_rkv, rwkv_w0, rwkv_w1, rwkv_w2, rwkv_a0, rwkv_a1, rwkv_a2, rwkv_v0, rwkv_v1, rwkv_v2, rwkv_g1, rwkv_g2, rwkv_k_k, rwkv_k_a, rwkv_r_k, rwkv_gn_g, rwkv_gn_b, rwkv_w_o):
    bsz, seq, d = x.shape
    depth = ln_g.shape[0]
    n_heads = d // RWKV_HEAD
    alpha = (2 * depth) ** 0.25
    h = x.reshape(bsz * seq, d)
    v_first = None
    for i in range(depth):
        j = i // 2
        if i % 2 == 1:
            h = _swap_token_order(h, bsz, seq)
        h = _ffn(h, ffn_w_in[i, 0].astype(BF16), ffn_w_out[i, 0].astype(BF16),
                 _row(ln_g[i, 0]), _row(ln_b[i, 0]), alpha)
        if i % 2 == 0:
            b_sp = jnp.broadcast_to(sgu_b_sp[j][:, :, None], sgu_b_sp[j].shape + (SGU_GROUP_DIM,))
            h = _sgu(h, sgu_w_in[j].astype(BF16), _row(sgu_b_in[j]), _row(sgu_ln_g[j]), _row(sgu_ln_b[j]),
                     sgu_w_sp[j], b_sp, sgu_w_out[j].astype(BF16), _row(ln_g[i, 1]), _row(ln_b[i, 1]), alpha)
        else:
            vres = None
            if v_first is not None:
                vres = (v_first, _row(rwkv_v0[j - 1]), rwkv_v1[j - 1].astype(BF16), rwkv_v2[j - 1].astype(BF16))
            r, w, k, v, a, g = _rwkv_proj(
                h, bsz, rwkv_mu[j], rwkv_w_rkv[j].astype(BF16), _row(rwkv_w0[j]),
                rwkv_w1[j].astype(BF16), rwkv_w2[j].astype(BF16), _row(rwkv_a0[j]),
                rwkv_a1[j].astype(BF16), rwkv_a2[j].astype(BF16),
                rwkv_g1[j].astype(BF16), rwkv_g2[j].astype(BF16), vres)
            if v_first is None:
                v_first = v
            pi = functools.partial(_param_inst, bsz=bsz, n_heads=n_heads)
            y = _rwkv_scan(r, w, k, v, a, bsz, pi(rwkv_k_k[j]), pi(rwkv_k_a[j]),
                           pi(rwkv_r_k[j].reshape(-1)), pi(rwkv_gn_g[j]), pi(rwkv_gn_b[j]))
            h = _rwkv_out(h, y, g, rwkv_w_o[j].astype(BF16), _row(ln_g[i, 1]), _row(ln_b[i, 1]), alpha)
        h = _ffn(h, ffn_w_in[i, 1].astype(BF16), ffn_w_out[i, 1].astype(BF16),
                 _row(ln_g[i, 2]), _row(ln_b[i, 2]), alpha)
        if i % 2 == 1:
            h = _swap_token_order(h, seq, bsz)
    return h.reshape(bsz, seq, d)
```

```python
import functools
import math

import jax
import jax.numpy as jnp
from jax import lax
from jax.experimental import pallas as pl
from jax.experimental.pallas import tpu as pltpu

F32 = jnp.float32
BF16 = jnp.bfloat16

LN_EPS = 1e-5
GN_EPS = 64e-5
SGU_CHUNK = 128
SGU_GROUP_DIM = 128
RWKV_HEAD = 64

VMEM_LIMIT_BYTES = 56 * 1024 * 1024

FFN_TM = 1024
FFN_SUB = 512
FFN_FC = 256
SGU_TM = 512
SGU_UC = 512
PROJ_TM = 512
OUT_TM = 512
SCAN_TT = 64


def _dot(a, b):
    return jnp.dot(a, b, preferred_element_type=F32)


def _layer_norm(z, g, b, eps=LN_EPS):
    mu = jnp.mean(z, axis=-1, keepdims=True)
    zc = z - mu
    var = jnp.mean(zc * zc, axis=-1, keepdims=True)
    return zc * lax.rsqrt(var + eps) * g + b


def _const_spec(shape):
    nd = len(shape)
    return pl.BlockSpec(shape, lambda i: (0,) * nd, pipeline_mode=pl.Buffered(1))


def _params(sem):
    return pltpu.CompilerParams(dimension_semantics=(sem,), vmem_limit_bytes=VMEM_LIMIT_BYTES)


def _ffn_rows(x, win_ref, wout_ref, g_ref, b_ref, act_ref, *, d_ff, fc, alpha):
    xb = x.astype(BF16)
    for c in range(d_ff // fc):
        lo = c * fc
        gate = _dot(xb, win_ref[:, lo:lo + fc])
        up = _dot(xb, win_ref[:, d_ff + lo:d_ff + lo + fc])
        act_ref[:, lo:lo + fc] = (gate * jax.nn.sigmoid(gate) * up).astype(BF16)
    o = _dot(act_ref[...], wout_ref[...])
    return _layer_norm(alpha * x + 0.5 * o, g_ref[...], b_ref[...])


def _ffn_body(x_ref, win_ref, wout_ref, g_ref, b_ref, o_ref, act_ref, *, sub, **kw):
    for s in range(x_ref.shape[0] // sub):
        rows = pl.ds(s * sub, sub)
        o_ref[rows, :] = _ffn_rows(x_ref[rows, :], win_ref, wout_ref, g_ref, b_ref, act_ref.at[rows, :], **kw)


def _ffn(x, w_in, w_out, ln_g, ln_b, alpha, tm=FFN_TM, fc=FFN_FC):
    t, d = x.shape
    d_ff = w_out.shape[0]
    tm = min(tm, t)
    fc = fc if d_ff % fc == 0 else d_ff
    body = functools.partial(_ffn_body, sub=min(FFN_SUB, tm), d_ff=d_ff, fc=fc, alpha=alpha)
    return pl.pallas_call(
        body,
        grid=(t // tm,),
        in_specs=[
            pl.BlockSpec((tm, d), lambda i: (i, 0)),
            _const_spec((d, 2 * d_ff)),
            _const_spec((d_ff, d)),
            _const_spec((1, d)),
            _const_spec((1, d)),
        ],
        out_specs=pl.BlockSpec((tm, d), lambda i: (i, 0)),
        out_shape=jax.ShapeDtypeStruct((t, d), F32),
        scratch_shapes=[pltpu.VMEM((tm, d_ff), BF16)],
        compiler_params=_params("parallel"),
        name="ffn",
    )(x, w_in, w_out, ln_g, ln_b)


def _gelu(z):
    return 0.5 * z * (1.0 + lax.erf(z * math.sqrt(0.5)))


def _sgu_body(x_ref, win_ref, bin_ref, lg_ref, lb_ref, wsp_ref, bsp_ref, wout_ref, g_ref, b_ref,
              o_ref, vn_ref, gated_ref, *, d_sgu, uc, alpha):
    tm = x_ref.shape[0]
    x = x_ref[...]
    xb = x.astype(BF16)
    v = _gelu(_dot(xb, win_ref[:, d_sgu:]) + bin_ref[:, d_sgu:])
    vn_ref[...] = _layer_norm(v, lg_ref[...], lb_ref[...]).astype(BF16)
    row = lax.broadcasted_iota(jnp.int32, (SGU_CHUNK, SGU_CHUNK), 0)
    col = lax.broadcasted_iota(jnp.int32, (SGU_CHUNK, SGU_CHUNK), 1)
    causal = col <= row
    groups_per_uc = uc // SGU_GROUP_DIM
    for c in range(d_sgu // uc):
        u = _gelu(_dot(xb, win_ref[:, c * uc:(c + 1) * uc]) + bin_ref[:, c * uc:(c + 1) * uc])
        for gi in range(groups_per_uc):
            g = c * groups_per_uc + gi
            w = jnp.where(causal, wsp_ref[g], 0.0).astype(BF16)
            lanes = slice(g * SGU_GROUP_DIM, (g + 1) * SGU_GROUP_DIM)
            for ci in range(tm // SGU_CHUNK):
                rows = slice(ci * SGU_CHUNK, (ci + 1) * SGU_CHUNK)
                mixed = _dot(w, vn_ref[rows, lanes]) + bsp_ref[g]
                ug = u[rows, gi * SGU_GROUP_DIM:(gi + 1) * SGU_GROUP_DIM]
                gated_ref[rows, lanes] = (ug * mixed).astype(BF16)
    o = _dot(gated_ref[...], wout_ref[...])
    o_ref[...] = _layer_norm(alpha * x + o, g_ref[...], b_ref[...])


def _sgu(x, w_in, b_in, sln_g, sln_b, w_sp, b_sp_bcast, w_out, ln_g, ln_b, alpha, tm=SGU_TM, uc=SGU_UC):
    t, d = x.shape
    d_sgu = w_out.shape[0]
    n_groups = w_sp.shape[0]
    tm = min(tm, t)
    body = functools.partial(_sgu_body, d_sgu=d_sgu, uc=uc, alpha=alpha)
    return pl.pallas_call(
        body,
        grid=(t // tm,),
        in_specs=[
            pl.BlockSpec((tm, d), lambda i: (i, 0)),
            _const_spec((d, 2 * d_sgu)),
            _const_spec((1, 2 * d_sgu)),
            _const_spec((1, d_sgu)),
            _const_spec((1, d_sgu)),
            _const_spec((n_groups, SGU_CHUNK, SGU_CHUNK)),
            _const_spec((n_groups, SGU_CHUNK, SGU_GROUP_DIM)),
            _const_spec((d_sgu, d)),
            _const_spec((1, d)),
            _const_spec((1, d)),
        ],
        out_specs=pl.BlockSpec((tm, d), lambda i: (i, 0)),
        out_shape=jax.ShapeDtypeStruct((t, d), F32),
        scratch_shapes=[pltpu.VMEM((tm, d_sgu), BF16), pltpu.VMEM((tm, d_sgu), BF16)],
        compiler_params=_params("parallel"),
        name="sgu",
    )(x, w_in, b_in, sln_g, sln_b, w_sp, b_sp_bcast, w_out, ln_g, ln_b)


def _softplus(z):
    return jnp.maximum(z, 0.0) + jnp.log1p(jnp.exp(-jnp.abs(z)))


def _proj_body(*refs, has_vres):
    if has_vres:
        (x_ref, xp_ref, mu_ref, wr_ref, wk_ref, wv_ref, w0_ref, w1_ref, w2_ref, a0_ref, a1_ref, a2_ref,
         g1_ref, g2_ref, vf_ref, v0_ref, v1_ref, v2_ref,
         r_out, w_out, k_out, v_out, a_out, g_out) = refs
    else:
        (x_ref, xp_ref, mu_ref, wr_ref, wk_ref, wv_ref, w0_ref, w1_ref, w2_ref, a0_ref, a1_ref, a2_ref,
         g1_ref, g2_ref,
         r_out, w_out, k_out, v_out, a_out, g_out) = refs
    bsz = xp_ref.shape[0]
    x = x_ref[...]
    prev = jnp.where(pl.program_id(0) == 0, 0.0, xp_ref[...])
    xx = jnp.concatenate([prev, x[:-bsz]], axis=0) - x

    def mix(n):
        return (x + xx * mu_ref[n:n + 1, :]).astype(BF16)

    r_out[...] = _dot(mix(0), wr_ref[...])
    k_out[...] = _dot(mix(1), wk_ref[...])
    xv = mix(2)
    v = _dot(xv, wv_ref[...])
    if has_vres:
        gate = jax.nn.sigmoid(v0_ref[...] + _dot(_dot(xv, v1_ref[...]).astype(BF16), v2_ref[...]))
        v = v + (vf_ref[...] - v) * gate
    v_out[...] = v
    lw = w0_ref[...] + _dot(jnp.tanh(_dot(mix(3), w1_ref[...])).astype(BF16), w2_ref[...])
    w_out[...] = -_softplus(-lw) - 0.5
    a_out[...] = jax.nn.sigmoid(a0_ref[...] + _dot(_dot(mix(4), a1_ref[...]).astype(BF16), a2_ref[...]))
    g_out[...] = _dot(jax.nn.sigmoid(_dot(mix(5), g1_ref[...])).astype(BF16), g2_ref[...])


def _rwkv_proj(x, bsz, mu, w_rkv, w0, w1, w2, a0, a1, a2, g1, g2, vres, tm=PROJ_TM):
    t, d = x.shape
    tm = min(tm, t)
    assert tm % bsz == 0 and bsz % 8 == 0
    has_vres = vres is not None
    tile = pl.BlockSpec((tm, d), lambda i: (i, 0))
    prev_rows = pl.BlockSpec((bsz, d), lambda i: (jnp.maximum(i * (tm // bsz) - 1, 0), 0))
    args = [x, x, mu, w_rkv[0], w_rkv[1], w_rkv[2], w0, w1, w2, a0, a1, a2, g1, g2]
    specs = [tile, prev_rows] + [_const_spec(a.shape) for a in args[2:]]
    if has_vres:
        v_first, v0, v1, v2 = vres
        args += [v_first, v0, v1, v2]
        specs += [tile, _const_spec(v0.shape), _const_spec(v1.shape), _const_spec(v2.shape)]
    body = functools.partial(_proj_body, has_vres=has_vres)
    return pl.pallas_call(
        body,
        grid=(t // tm,),
        in_specs=specs,
        out_specs=[tile] * 6,
        out_shape=[jax.ShapeDtypeStruct((t, d), F32)] * 6,
        compiler_params=_params("parallel"),
        name="rwkv_proj",
    )(*args)


_AV, _RW, _DEC, _BV, _K2, _V = range(6)
_C_B, _C_K, _C_BONUS = range(3)


def _scan_body(r_ref, w_ref, k_ref, v_ref, a_ref, kk_ref, ka_ref, rk_ref, gg_ref, gb_ref,
               o_ref, s_ref, ops_ref, crow_ref):
    n = RWKV_HEAD
    half = 64
    n_pairs = r_ref.shape[1] // (2 * n)
    bsz = half // n_pairs
    tt = r_ref.shape[0] // bsz
    lane = lax.broadcasted_iota(jnp.int32, (n, 2 * half), 1)
    low = lane < half

    @pl.when(pl.program_id(0) == 0)
    def _():
        s_ref[...] = jnp.zeros_like(s_ref)

    def load_step(x_ref, t):
        rows = x_ref[pl.ds(pl.multiple_of(t * bsz, bsz), bsz), :]
        return jnp.concatenate([rows[:, hp * 2 * n:(hp + 1) * 2 * n] for hp in range(n_pairs)], axis=0)

    def load_pair(x_ref, t):
        tr = jnp.concatenate([load_step(x_ref, t), load_step(x_ref, t + 1)], axis=0).T
        top, bot = tr[:n], tr[n:]
        return (jnp.where(low, top, pltpu.roll(bot, half, axis=1)),
                jnp.where(low, pltpu.roll(top, half, axis=1), bot))

    def prep(t, slot):
        rs, ws, ks, vs, gs = (load_pair(x, t) for x in (r_ref, w_ref, k_ref, v_ref, a_ref))
        for st in range(2):
            r, k, v, a = rs[st], ks[st], vs[st], gs[st]
            kk = k * kk_ref[...]
            norm = jnp.sqrt(jnp.sum(kk * kk, axis=0, keepdims=True))
            kk = kk / jnp.maximum(norm, 1e-12)
            k2 = k * (1.0 + (a - 1.0) * ka_ref[...])
            dec = jnp.exp(-jnp.exp(ws[st]))
            bv = kk * a
            rk2 = r * k2
            ops_ref[slot, st, _AV] = -kk
            ops_ref[slot, st, _RW] = r * dec
            ops_ref[slot, st, _DEC] = dec
            ops_ref[slot, st, _BV] = bv
            ops_ref[slot, st, _K2] = k2
            ops_ref[slot, st, _V] = v
            crow_ref[slot, st, _C_B:_C_B + 1] = jnp.sum(r * bv, axis=0, keepdims=True)
            crow_ref[slot, st, _C_K:_C_K + 1] = jnp.sum(rk2, axis=0, keepdims=True)
            crow_ref[slot, st, _C_BONUS:_C_BONUS + 1] = jnp.sum(rk2 * rk_ref[...], axis=0, keepdims=True)

    def row(slot, st, which, j):
        return ops_ref[slot, st, which, pl.ds(j, 1), :]

    def reduce_only(slot, st):
        sa = jnp.zeros((n, 2 * half), F32)
        ya = jnp.zeros((n, 2 * half), F32)
        for j in range(n):
            s = s_ref[j]
            sa = sa + s * row(slot, st, _AV, j)
            ya = ya + s * row(slot, st, _RW, j)
        return sa, ya

    def advance(sa, ya, upd, nxt):
        us, ut = upd
        ns, nt = nxt
        v = ops_ref[us, ut, _V]
        y = (ya + sa * crow_ref[us, ut, _C_B:_C_B + 1] + v * crow_ref[us, ut, _C_K:_C_K + 1])
        mean = jnp.mean(y, axis=0, keepdims=True)
        yc = y - mean
        var = jnp.mean(yc * yc, axis=0, keepdims=True)
        out = (yc * lax.rsqrt(var + GN_EPS) * gg_ref[...] + gb_ref[...]
               + crow_ref[us, ut, _C_BONUS:_C_BONUS + 1] * v)
        sa_n = jnp.zeros_like(sa)
        ya_n = jnp.zeros_like(ya)
        for j in range(n):
            s = s_ref[j] * row(us, ut, _DEC, j) + sa * row(us, ut, _BV, j) + v * row(us, ut, _K2, j)
            s_ref[j] = s
            sa_n = sa_n + s * row(ns, nt, _AV, j)
            ya_n = ya_n + s * row(ns, nt, _RW, j)
        return out, sa_n, ya_n

    def store_pair(t, y0, y1):
        top = jnp.where(low, y0, pltpu.roll(y1, half, axis=1))
        bot = jnp.where(low, pltpu.roll(y0, half, axis=1), y1)
        back = jnp.concatenate([top, bot], axis=0).T
        for st in range(2):
            rows = jnp.concatenate(
                [back[st * half + hp * bsz:st * half + (hp + 1) * bsz] for hp in range(n_pairs)], axis=1)
            o_ref[pl.ds(pl.multiple_of((t + st) * bsz, bsz), bsz), :] = rows

    prep(0, 0)
    sa0, ya0 = reduce_only(0, 0)

    def quad(q, carry):
        sa, ya = carry
        t = q * 4
        prep(t + 2, 1)
        y0, sa, ya = advance(sa, ya, (0, 0), (0, 1))
        y1, sa, ya = advance(sa, ya, (0, 1), (1, 0))
        store_pair(t, y0, y1)
        prep(jnp.minimum(t + 4, tt - 2), 0)
        y2, sa, ya = advance(sa, ya, (1, 0), (1, 1))
        y3, sa, ya = advance(sa, ya, (1, 1), (0, 0))
        store_pair(t + 2, y2, y3)
        return sa, ya

    lax.fori_loop(0, tt // 4, quad, (sa0, ya0))


def _rwkv_scan(r, w, k, v, a, bsz, kk_p, ka_p, rk_p, gg_p, gb_p, tt=SCAN_TT):
    rows, d = r.shape
    s = rows // bsz
    tt = min(tt, s)
    n = RWKV_HEAD
    lanes = 2 * n
    assert tt % 4 == 0 and bsz % 8 == 0 and bsz * (d // lanes) == 64
    tile = pl.BlockSpec((tt * bsz, d), lambda i: (i, 0))
    par = _const_spec((n, lanes))
    return pl.pallas_call(
        _scan_body,
        grid=(s // tt,),
        in_specs=[tile] * 5 + [par] * 5,
        out_specs=tile,
        out_shape=jax.ShapeDtypeStruct(r.shape, F32),
        scratch_shapes=[pltpu.VMEM((n, n, lanes), F32),
                        pltpu.VMEM((2, 2, 6, n, lanes), F32),
                        pltpu.VMEM((2, 2, 8, lanes), F32)],
        compiler_params=_params("arbitrary"),
        name="rwkv_scan",
    )(r, w, k, v, a, kk_p, ka_p, rk_p, gg_p, gb_p)


def _out_ffn_body(x_ref, y_ref, g_ref, wo_ref, lg_ref, lb_ref, win_ref, wout_ref, fg_ref, fb_ref,
                  o_ref, act_ref, *, sub, alpha, **kw):
    for s in range(x_ref.shape[0] // sub):
        rows = pl.ds(s * sub, sub)
        o = _dot((y_ref[rows, :] * g_ref[rows, :]).astype(BF16), wo_ref[...])
        h = _layer_norm(alpha * x_ref[rows, :] + o, lg_ref[...], lb_ref[...])
        o_ref[rows, :] = _ffn_rows(h, win_ref, wout_ref, fg_ref, fb_ref, act_ref.at[rows, :], alpha=alpha, **kw)


def _rwkv_out_ffn(x, y, g, w_o, ln_g, ln_b, w_in, w_out, fln_g, fln_b, alpha, tm=OUT_TM, fc=FFN_FC):
    t, d = x.shape
    d_ff = w_out.shape[0]
    tm = min(tm, t)
    fc = fc if d_ff % fc == 0 else d_ff
    tile = pl.BlockSpec((tm, d), lambda i: (i, 0))
    body = functools.partial(_out_ffn_body, sub=min(FFN_SUB, tm), d_ff=d_ff, fc=fc, alpha=alpha)
    return pl.pallas_call(
        body,
        grid=(t // tm,),
        in_specs=[tile, tile, tile, _const_spec((d, d)), _const_spec((1, d)), _const_spec((1, d)),
                  _const_spec((d, 2 * d_ff)), _const_spec((d_ff, d)), _const_spec((1, d)), _const_spec((1, d))],
        out_specs=tile,
        out_shape=jax.ShapeDtypeStruct((t, d), F32),
        scratch_shapes=[pltpu.VMEM((tm, d_ff), BF16)],
        compiler_params=_params("parallel"),
        name="rwkv_out_ffn",
    )(x, y, g, w_o, ln_g, ln_b, w_in, w_out, fln_g, fln_b)


def _param_inst(p, bsz, n_heads):
    n_pairs = n_heads // 2
    t = p.reshape(n_pairs, 2, RWKV_HEAD).transpose(2, 1, 0)
    t = jnp.broadcast_to(t[:, :, :, None], (RWKV_HEAD, 2, n_pairs, bsz))
    return t.reshape(RWKV_HEAD, 2 * n_pairs * bsz)


def _swap_token_order(h, rows_major, rows_minor):
    d = h.shape[-1]
    return h.reshape(rows_major, rows_minor, d).transpose(1, 0, 2).reshape(rows_major * rows_minor, d)


def _row(p):
    return p.reshape(1, -1)


def kernel(x, ln_g, ln_b, ffn_w_in, ffn_w_out, sgu_w_in, sgu_b_in, sgu_ln_g, sgu_ln_b, sgu_w_sp, sgu_b_sp, sgu_w_out, rwkv_mu, rwkv_w_rkv, rwkv_w0, rwkv_w1, rwkv_w2, rwkv_a0, rwkv_a1, rwkv_a2, rwkv_v0, rwkv_v1, rwkv_v2, rwkv_g1, rwkv_g2, rwkv_k_k, rwkv_k_a, rwkv_r_k, rwkv_gn_g, rwkv_gn_b, rwkv_w_o):
    bsz, seq, d = x.shape
    depth = ln_g.shape[0]
    n_heads = d // RWKV_HEAD
    alpha = (2 * depth) ** 0.25
    h = x.reshape(bsz * seq, d)
    v_first = None
    for i in range(depth):
        j = i // 2
        if i % 2 == 1:
            h = _swap_token_order(h, bsz, seq)
        h = _ffn(h, ffn_w_in[i, 0].astype(BF16), ffn_w_out[i, 0].astype(BF16),
                 _row(ln_g[i, 0]), _row(ln_b[i, 0]), alpha)
        if i % 2 == 0:
            b_sp = jnp.broadcast_to(sgu_b_sp[j][:, :, None], sgu_b_sp[j].shape + (SGU_GROUP_DIM,))
            h = _sgu(h, sgu_w_in[j].astype(BF16), _row(sgu_b_in[j]), _row(sgu_ln_g[j]), _row(sgu_ln_b[j]),
                     sgu_w_sp[j], b_sp, sgu_w_out[j].astype(BF16), _row(ln_g[i, 1]), _row(ln_b[i, 1]), alpha)
        else:
            vres = None
            if v_first is not None:
                vres = (v_first, _row(rwkv_v0[j - 1]), rwkv_v1[j - 1].astype(BF16), rwkv_v2[j - 1].astype(BF16))
            r, w, k, v, a, g = _rwkv_proj(
                h, bsz, rwkv_mu[j], rwkv_w_rkv[j].astype(BF16), _row(rwkv_w0[j]),
                rwkv_w1[j].astype(BF16), rwkv_w2[j].astype(BF16), _row(rwkv_a0[j]),
                rwkv_a1[j].astype(BF16), rwkv_a2[j].astype(BF16),
                rwkv_g1[j].astype(BF16), rwkv_g2[j].astype(BF16), vres)
            if v_first is None:
                v_first = v
            pi = functools.partial(_param_inst, bsz=bsz, n_heads=n_heads)
            y = _rwkv_scan(r, w, k, v, a, bsz, pi(rwkv_k_k[j]), pi(rwkv_k_a[j]),
                           pi(rwkv_r_k[j].reshape(-1)), pi(rwkv_gn_g[j]), pi(rwkv_gn_b[j]))
            h = _rwkv_out_ffn(h, y, g, rwkv_w_o[j].astype(BF16), _row(ln_g[i, 1]), _row(ln_b[i, 1]),
                              ffn_w_in[i, 1].astype(BF16), ffn_w_out[i, 1].astype(BF16),
                              _row(ln_g[i, 2]), _row(ln_b[i, 2]), alpha)
            h = _swap_token_order(h, seq, bsz)
        if i % 2 == 0:
            h = _ffn(h, ffn_w_in[i, 1].astype(BF16), ffn_w_out[i, 1].astype(BF16),
                     _row(ln_g[i, 2]), _row(ln_b[i, 2]), alpha)
    return h.reshape(bsz, seq, d)
```

```python
import functools
import math
from typing import NamedTuple

import jax
import jax.numpy as jnp
from jax import lax
from jax.experimental import pallas as pl
from jax.experimental.pallas import tpu as pltpu

F32 = jnp.float32
BF16 = jnp.bfloat16

LN_EPS = 1e-5
GN_EPS = 64e-5
SGU_CHUNK = 128
SGU_GROUP_DIM = 128
RWKV_HEAD = 64

VMEM_LIMIT_BYTES = 56 * 1024 * 1024

FFN_TM = 1024
FFN_SUB = 512
FFN_FC = 256
SGU_TM = 512
SGU_UC = 512
PROJ_TM = 512
OUT_TM = 512
SCAN_TT = 64


def _dot(a, b):
    return jnp.dot(a, b, preferred_element_type=F32)


def _layer_norm(z, g, b, eps=LN_EPS):
    mu = jnp.mean(z, axis=-1, keepdims=True)
    zc = z - mu
    var = jnp.mean(zc * zc, axis=-1, keepdims=True)
    return zc * lax.rsqrt(var + eps) * g + b


def _const_spec(shape):
    nd = len(shape)
    return pl.BlockSpec(shape, lambda i: (0,) * nd, pipeline_mode=pl.Buffered(1))


class _Sel(NamedTuple):
    arr: jax.Array
    idx: tuple

    @property
    def shape(self):
        return self.arr.shape[len(self.idx):]

    def spec(self):
        shape, idx = self.shape, tuple(self.idx)
        return pl.BlockSpec((None,) * len(idx) + shape, lambda i: idx + (0,) * len(shape),
                            pipeline_mode=pl.Buffered(1))


def _params(sem):
    return pltpu.CompilerParams(dimension_semantics=(sem,), vmem_limit_bytes=VMEM_LIMIT_BYTES)


def _ffn_rows(x, win_ref, wout_ref, g_ref, b_ref, act_ref, *, d_ff, fc, alpha):
    xb = x.astype(BF16)
    for c in range(d_ff // fc):
        lo = c * fc
        gate = _dot(xb, win_ref[:, lo:lo + fc])
        up = _dot(xb, win_ref[:, d_ff + lo:d_ff + lo + fc])
        act_ref[:, lo:lo + fc] = (gate * jax.nn.sigmoid(gate) * up).astype(BF16)
    o = _dot(act_ref[...], wout_ref[...])
    return _layer_norm(alpha * x + 0.5 * o, g_ref[...], b_ref[...])


def _ffn_body(x_ref, win_ref, wout_ref, g_ref, b_ref, o_ref, act_ref, *, sub, **kw):
    for s in range(x_ref.shape[0] // sub):
        rows = pl.ds(s * sub, sub)
        o_ref[rows, :] = _ffn_rows(x_ref[rows, :], win_ref, wout_ref, g_ref, b_ref, act_ref.at[rows, :], **kw)


def _ffn(x, w_in, w_out, ln_g, ln_b, alpha, tm=FFN_TM, fc=FFN_FC):
    t, d = x.shape
    d_ff = w_out.shape[0]
    tm = min(tm, t)
    fc = fc if d_ff % fc == 0 else d_ff
    body = functools.partial(_ffn_body, sub=min(FFN_SUB, tm), d_ff=d_ff, fc=fc, alpha=alpha)
    return pl.pallas_call(
        body,
        grid=(t // tm,),
        in_specs=[
            pl.BlockSpec((tm, d), lambda i: (i, 0)),
            w_in.spec(),
            w_out.spec(),
            _const_spec((1, d)),
            _const_spec((1, d)),
        ],
        out_specs=pl.BlockSpec((tm, d), lambda i: (i, 0)),
        out_shape=jax.ShapeDtypeStruct((t, d), F32),
        scratch_shapes=[pltpu.VMEM((tm, d_ff), BF16)],
        compiler_params=_params("parallel"),
        name="ffn",
    )(x, w_in.arr, w_out.arr, ln_g, ln_b)


def _gelu(z):
    return 0.5 * z * (1.0 + lax.erf(z * math.sqrt(0.5)))


def _sgu_body(x_ref, win_ref, bin_ref, lg_ref, lb_ref, wsp_ref, bsp_ref, wout_ref, g_ref, b_ref,
              o_ref, vn_ref, gated_ref, *, d_sgu, uc, alpha):
    tm = x_ref.shape[0]
    x = x_ref[...]
    xb = x.astype(BF16)
    v = _gelu(_dot(xb, win_ref[:, d_sgu:]) + bin_ref[:, d_sgu:])
    vn_ref[...] = _layer_norm(v, lg_ref[...], lb_ref[...]).astype(BF16)
    row = lax.broadcasted_iota(jnp.int32, (SGU_CHUNK, SGU_CHUNK), 0)
    col = lax.broadcasted_iota(jnp.int32, (SGU_CHUNK, SGU_CHUNK), 1)
    causal = col <= row
    groups_per_uc = uc // SGU_GROUP_DIM
    for c in range(d_sgu // uc):
        u = _gelu(_dot(xb, win_ref[:, c * uc:(c + 1) * uc]) + bin_ref[:, c * uc:(c + 1) * uc])
        for gi in range(groups_per_uc):
            g = c * groups_per_uc + gi
            w = jnp.where(causal, wsp_ref[g], 0.0).astype(BF16)
            lanes = slice(g * SGU_GROUP_DIM, (g + 1) * SGU_GROUP_DIM)
            for ci in range(tm // SGU_CHUNK):
                rows = slice(ci * SGU_CHUNK, (ci + 1) * SGU_CHUNK)
                mixed = _dot(w, vn_ref[rows, lanes]) + bsp_ref[g]
                ug = u[rows, gi * SGU_GROUP_DIM:(gi + 1) * SGU_GROUP_DIM]
                gated_ref[rows, lanes] = (ug * mixed).astype(BF16)
    o = _dot(gated_ref[...], wout_ref[...])
    o_ref[...] = _layer_norm(alpha * x + o, g_ref[...], b_ref[...])


def _sgu(x, w_in, b_in, sln_g, sln_b, w_sp, b_sp_bcast, w_out, ln_g, ln_b, alpha, tm=SGU_TM, uc=SGU_UC):
    t, d = x.shape
    d_sgu = w_out.shape[0]
    n_groups = w_sp.shape[0]
    tm = min(tm, t)
    body = functools.partial(_sgu_body, d_sgu=d_sgu, uc=uc, alpha=alpha)
    return pl.pallas_call(
        body,
        grid=(t // tm,),
        in_specs=[
            pl.BlockSpec((tm, d), lambda i: (i, 0)),
            w_in.spec(),
            _const_spec((1, 2 * d_sgu)),
            _const_spec((1, d_sgu)),
            _const_spec((1, d_sgu)),
            _const_spec((n_groups, SGU_CHUNK, SGU_CHUNK)),
            _const_spec((n_groups, SGU_CHUNK, SGU_GROUP_DIM)),
            w_out.spec(),
            _const_spec((1, d)),
            _const_spec((1, d)),
        ],
        out_specs=pl.BlockSpec((tm, d), lambda i: (i, 0)),
        out_shape=jax.ShapeDtypeStruct((t, d), F32),
        scratch_shapes=[pltpu.VMEM((tm, d_sgu), BF16), pltpu.VMEM((tm, d_sgu), BF16)],
        compiler_params=_params("parallel"),
        name="sgu",
    )(x, w_in.arr, b_in, sln_g, sln_b, w_sp, b_sp_bcast, w_out.arr, ln_g, ln_b)


def _softplus(z):
    return jnp.maximum(z, 0.0) + jnp.log1p(jnp.exp(-jnp.abs(z)))


def _proj_body(*refs, has_vres):
    if has_vres:
        (x_ref, xp_ref, mu_ref, wr_ref, wk_ref, wv_ref, w0_ref, w1_ref, w2_ref, a0_ref, a1_ref, a2_ref,
         g1_ref, g2_ref, vf_ref, v0_ref, v1_ref, v2_ref,
         r_out, w_out, k_out, v_out, a_out, g_out) = refs
    else:
        (x_ref, xp_ref, mu_ref, wr_ref, wk_ref, wv_ref, w0_ref, w1_ref, w2_ref, a0_ref, a1_ref, a2_ref,
         g1_ref, g2_ref,
         r_out, w_out, k_out, v_out, a_out, g_out) = refs
    bsz = xp_ref.shape[0]
    x = x_ref[...]
    prev = jnp.where(pl.program_id(0) == 0, 0.0, xp_ref[...])
    xx = jnp.concatenate([prev, x[:-bsz]], axis=0) - x

    def mix(n):
        return (x + xx * mu_ref[n:n + 1, :]).astype(BF16)

    r_out[...] = _dot(mix(0), wr_ref[...])
    k_out[...] = _dot(mix(1), wk_ref[...])
    xv = mix(2)
    v = _dot(xv, wv_ref[...])
    if has_vres:
        gate = jax.nn.sigmoid(v0_ref[...] + _dot(_dot(xv, v1_ref[...]).astype(BF16), v2_ref[...]))
        v = v + (vf_ref[...] - v) * gate
    v_out[...] = v
    lw = w0_ref[...] + _dot(jnp.tanh(_dot(mix(3), w1_ref[...])).astype(BF16), w2_ref[...])
    w_out[...] = -_softplus(-lw) - 0.5
    a_out[...] = jax.nn.sigmoid(a0_ref[...] + _dot(_dot(mix(4), a1_ref[...]).astype(BF16), a2_ref[...]))
    g_out[...] = _dot(jax.nn.sigmoid(_dot(mix(5), g1_ref[...])).astype(BF16), g2_ref[...])


def _rwkv_proj(x, bsz, mu, w_rkv, w0, w1, w2, a0, a1, a2, g1, g2, vres, tm=PROJ_TM):
    t, d = x.shape
    tm = min(tm, t)
    assert tm % bsz == 0 and bsz % 8 == 0
    has_vres = vres is not None
    tile = pl.BlockSpec((tm, d), lambda i: (i, 0))
    prev_rows = pl.BlockSpec((bsz, d), lambda i: (jnp.maximum(i * (tm // bsz) - 1, 0), 0))
    small = [w0, w1, w2, a0, a1, a2, g1, g2]
    args = [x, x, mu] + [w.arr for w in w_rkv] + small
    specs = ([tile, prev_rows, _const_spec(mu.shape)] + [w.spec() for w in w_rkv]
             + [_const_spec(a.shape) for a in small])
    if has_vres:
        v_first, v0, v1, v2 = vres
        args += [v_first, v0, v1, v2]
        specs += [tile, _const_spec(v0.shape), _const_spec(v1.shape), _const_spec(v2.shape)]
    body = functools.partial(_proj_body, has_vres=has_vres)
    return pl.pallas_call(
        body,
        grid=(t // tm,),
        in_specs=specs,
        out_specs=[tile] * 6,
        out_shape=[jax.ShapeDtypeStruct((t, d), F32)] * 6,
        compiler_params=_params("parallel"),
        name="rwkv_proj",
    )(*args)


_AV, _RW, _DEC, _BV, _K2, _V = range(6)
_C_B, _C_K, _C_BONUS = range(3)


def _scan_body(r_ref, w_ref, k_ref, v_ref, a_ref, kk_ref, ka_ref, rk_ref, gg_ref, gb_ref,
               o_ref, s_ref, ops_ref, crow_ref):
    n = RWKV_HEAD
    half = 64
    n_pairs = r_ref.shape[1] // (2 * n)
    bsz = half // n_pairs
    tt = r_ref.shape[0] // bsz
    lane = lax.broadcasted_iota(jnp.int32, (n, 2 * half), 1)
    low = lane < half

    @pl.when(pl.program_id(0) == 0)
    def _():
        s_ref[...] = jnp.zeros_like(s_ref)

    def load_step(x_ref, t):
        rows = x_ref[pl.ds(pl.multiple_of(t * bsz, bsz), bsz), :]
        return jnp.concatenate([rows[:, hp * 2 * n:(hp + 1) * 2 * n] for hp in range(n_pairs)], axis=0)

    def load_pair(x_ref, t):
        tr = jnp.concatenate([load_step(x_ref, t), load_step(x_ref, t + 1)], axis=0).T
        top, bot = tr[:n], tr[n:]
        return (jnp.where(low, top, pltpu.roll(bot, half, axis=1)),
                jnp.where(low, pltpu.roll(top, half, axis=1), bot))

    def prep(t, slot):
        rs, ws, ks, vs, gs = (load_pair(x, t) for x in (r_ref, w_ref, k_ref, v_ref, a_ref))
        for st in range(2):
            r, k, v, a = rs[st], ks[st], vs[st], gs[st]
            kk = k * kk_ref[...]
            norm = jnp.sqrt(jnp.sum(kk * kk, axis=0, keepdims=True))
            kk = kk / jnp.maximum(norm, 1e-12)
            k2 = k * (1.0 + (a - 1.0) * ka_ref[...])
            dec = jnp.exp(-jnp.exp(ws[st]))
            bv = kk * a
            rk2 = r * k2
            ops_ref[slot, st, _AV] = -kk
            ops_ref[slot, st, _RW] = r * dec
            ops_ref[slot, st, _DEC] = dec
            ops_ref[slot, st, _BV] = bv
            ops_ref[slot, st, _K2] = k2
            ops_ref[slot, st, _V] = v
            crow_ref[slot, st, _C_B:_C_B + 1] = jnp.sum(r * bv, axis=0, keepdims=True)
            crow_ref[slot, st, _C_K:_C_K + 1] = jnp.sum(rk2, axis=0, keepdims=True)
            crow_ref[slot, st, _C_BONUS:_C_BONUS + 1] = jnp.sum(rk2 * rk_ref[...], axis=0, keepdims=True)

    def row(slot, st, which, j):
        return ops_ref[slot, st, which, pl.ds(j, 1), :]

    def reduce_only(slot, st):
        sa = jnp.zeros((n, 2 * half), F32)
        ya = jnp.zeros((n, 2 * half), F32)
        for j in range(n):
            s = s_ref[j]
            sa = sa + s * row(slot, st, _AV, j)
            ya = ya + s * row(slot, st, _RW, j)
        return sa, ya

    def advance(sa, ya, upd, nxt):
        us, ut = upd
        ns, nt = nxt
        v = ops_ref[us, ut, _V]
        y = (ya + sa * crow_ref[us, ut, _C_B:_C_B + 1] + v * crow_ref[us, ut, _C_K:_C_K + 1])
        mean = jnp.mean(y, axis=0, keepdims=True)
        yc = y - mean
        var = jnp.mean(yc * yc, axis=0, keepdims=True)
        out = (yc * lax.rsqrt(var + GN_EPS) * gg_ref[...] + gb_ref[...]
               + crow_ref[us, ut, _C_BONUS:_C_BONUS + 1] * v)
        sa_n = jnp.zeros_like(sa)
        ya_n = jnp.zeros_like(ya)
        for j in range(n):
            s = s_ref[j] * row(us, ut, _DEC, j) + sa * row(us, ut, _BV, j) + v * row(us, ut, _K2, j)
            s_ref[j] = s
            sa_n = sa_n + s * row(ns, nt, _AV, j)
            ya_n = ya_n + s * row(ns, nt, _RW, j)
        return out, sa_n, ya_n

    def store_pair(t, y0, y1):
        top = jnp.where(low, y0, pltpu.roll(y1, half, axis=1))
        bot = jnp.where(low, pltpu.roll(y0, half, axis=1), y1)
        back = jnp.concatenate([top, bot], axis=0).T
        for st in range(2):
            rows = jnp.concatenate(
                [back[st * half + hp * bsz:st * half + (hp + 1) * bsz] for hp in range(n_pairs)], axis=1)
            o_ref[pl.ds(pl.multiple_of((t + st) * bsz, bsz), bsz), :] = rows

    prep(0, 0)
    sa0, ya0 = reduce_only(0, 0)

    def quad(q, carry):
        sa, ya = carry
        t = q * 4
        prep(t + 2, 1)
        y0, sa, ya = advance(sa, ya, (0, 0), (0, 1))
        y1, sa, ya = advance(sa, ya, (0, 1), (1, 0))
        store_pair(t, y0, y1)
        prep(jnp.minimum(t + 4, tt - 2), 0)
        y2, sa, ya = advance(sa, ya, (1, 0), (1, 1))
        y3, sa, ya = advance(sa, ya, (1, 1), (0, 0))
        store_pair(t + 2, y2, y3)
        return sa, ya

    lax.fori_loop(0, tt // 4, quad, (sa0, ya0))


def _rwkv_scan(r, w, k, v, a, bsz, kk_p, ka_p, rk_p, gg_p, gb_p, tt=SCAN_TT):
    rows, d = r.shape
    s = rows // bsz
    tt = min(tt, s)
    n = RWKV_HEAD
    lanes = 2 * n
    assert tt % 4 == 0 and bsz % 8 == 0 and bsz * (d // lanes) == 64
    tile = pl.BlockSpec((tt * bsz, d), lambda i: (i, 0))
    par = _const_spec((n, lanes))
    return pl.pallas_call(
        _scan_body,
        grid=(s // tt,),
        in_specs=[tile] * 5 + [par] * 5,
        out_specs=tile,
        out_shape=jax.ShapeDtypeStruct(r.shape, F32),
        scratch_shapes=[pltpu.VMEM((n, n, lanes), F32),
                        pltpu.VMEM((2, 2, 6, n, lanes), F32),
                        pltpu.VMEM((2, 2, 8, lanes), F32)],
        compiler_params=_params("arbitrary"),
        name="rwkv_scan",
    )(r, w, k, v, a, kk_p, ka_p, rk_p, gg_p, gb_p)


def _out_ffn_body(x_ref, y_ref, g_ref, wo_ref, lg_ref, lb_ref, win_ref, wout_ref, fg_ref, fb_ref,
                  o_ref, act_ref, *, sub, alpha, **kw):
    for s in range(x_ref.shape[0] // sub):
        rows = pl.ds(s * sub, sub)
        o = _dot((y_ref[rows, :] * g_ref[rows, :]).astype(BF16), wo_ref[...])
        h = _layer_norm(alpha * x_ref[rows, :] + o, lg_ref[...], lb_ref[...])
        o_ref[rows, :] = _ffn_rows(h, win_ref, wout_ref, fg_ref, fb_ref, act_ref.at[rows, :], alpha=alpha, **kw)


def _rwkv_out_ffn(x, y, g, w_o, ln_g, ln_b, w_in, w_out, fln_g, fln_b, alpha, tm=OUT_TM, fc=FFN_FC):
    t, d = x.shape
    d_ff = w_out.shape[0]
    tm = min(tm, t)
    fc = fc if d_ff % fc == 0 else d_ff
    tile = pl.BlockSpec((tm, d), lambda i: (i, 0))
    body = functools.partial(_out_ffn_body, sub=min(FFN_SUB, tm), d_ff=d_ff, fc=fc, alpha=alpha)
    return pl.pallas_call(
        body,
        grid=(t // tm,),
        in_specs=[tile, tile, tile, w_o.spec(), _const_spec((1, d)), _const_spec((1, d)),
                  w_in.spec(), w_out.spec(), _const_spec((1, d)), _const_spec((1, d))],
        out_specs=tile,
        out_shape=jax.ShapeDtypeStruct((t, d), F32),
        scratch_shapes=[pltpu.VMEM((tm, d_ff), BF16)],
        compiler_params=_params("parallel"),
        name="rwkv_out_ffn",
    )(x, y, g, w_o.arr, ln_g, ln_b, w_in.arr, w_out.arr, fln_g, fln_b)


def _param_inst(p, bsz, n_heads):
    n_pairs = n_heads // 2
    t = p.reshape(n_pairs, 2, RWKV_HEAD).transpose(2, 1, 0)
    t = jnp.broadcast_to(t[:, :, :, None], (RWKV_HEAD, 2, n_pairs, bsz))
    return t.reshape(RWKV_HEAD, 2 * n_pairs * bsz)


def _swap_token_order(h, rows_major, rows_minor):
    d = h.shape[-1]
    return h.reshape(rows_major, rows_minor, d).transpose(1, 0, 2).reshape(rows_major * rows_minor, d)


def _row(p):
    return p.reshape(1, -1)


def kernel(x, ln_g, ln_b, ffn_w_in, ffn_w_out, sgu_w_in, sgu_b_in, sgu_ln_g, sgu_ln_b, sgu_w_sp, sgu_b_sp, sgu_w_out, rwkv_mu, rwkv_w_rkv, rwkv_w0, rwkv_w1, rwkv_w2, rwkv_a0, rwkv_a1, rwkv_a2, rwkv_v0, rwkv_v1, rwkv_v2, rwkv_g1, rwkv_g2, rwkv_k_k, rwkv_k_a, rwkv_r_k, rwkv_gn_g, rwkv_gn_b, rwkv_w_o):
    bsz, seq, d = x.shape
    depth = ln_g.shape[0]
    n_heads = d // RWKV_HEAD
    alpha = (2 * depth) ** 0.25
    h = x.reshape(bsz * seq, d)
    ffn_in, ffn_out = ffn_w_in.astype(BF16), ffn_w_out.astype(BF16)
    sgu_in, sgu_out = sgu_w_in.astype(BF16), sgu_w_out.astype(BF16)
    w_rkv_b, w_o_b = rwkv_w_rkv.astype(BF16), rwkv_w_o.astype(BF16)
    v_first = None
    for i in range(depth):
        j = i // 2
        if i % 2 == 1:
            h = _swap_token_order(h, bsz, seq)
        h = _ffn(h, _Sel(ffn_in, (i, 0)), _Sel(ffn_out, (i, 0)), _row(ln_g[i, 0]), _row(ln_b[i, 0]), alpha)
        if i % 2 == 0:
            b_sp = jnp.broadcast_to(sgu_b_sp[j][:, :, None], sgu_b_sp[j].shape + (SGU_GROUP_DIM,))
            h = _sgu(h, _Sel(sgu_in, (j,)), _row(sgu_b_in[j]), _row(sgu_ln_g[j]), _row(sgu_ln_b[j]),
                     sgu_w_sp[j], b_sp, _Sel(sgu_out, (j,)), _row(ln_g[i, 1]), _row(ln_b[i, 1]), alpha)
        else:
            vres = None
            if v_first is not None:
                vres = (v_first, _row(rwkv_v0[j - 1]), rwkv_v1[j - 1].astype(BF16), rwkv_v2[j - 1].astype(BF16))
            r, w, k, v, a, g = _rwkv_proj(
                h, bsz, rwkv_mu[j], [_Sel(w_rkv_b, (j, n)) for n in range(3)], _row(rwkv_w0[j]),
                rwkv_w1[j].astype(BF16), rwkv_w2[j].astype(BF16), _row(rwkv_a0[j]),
                rwkv_a1[j].astype(BF16), rwkv_a2[j].astype(BF16),
                rwkv_g1[j].astype(BF16), rwkv_g2[j].astype(BF16), vres)
            if v_first is None:
                v_first = v
            pi = functools.partial(_param_inst, bsz=bsz, n_heads=n_heads)
            y = _rwkv_scan(r, w, k, v, a, bsz, pi(rwkv_k_k[j]), pi(rwkv_k_a[j]),
                           pi(rwkv_r_k[j].reshape(-1)), pi(rwkv_gn_g[j]), pi(rwkv_gn_b[j]))
            h = _rwkv_out_ffn(h, y, g, _Sel(w_o_b, (j,)), _row(ln_g[i, 1]), _row(ln_b[i, 1]),
                              _Sel(ffn_in, (i, 1)), _Sel(ffn_out, (i, 1)),
                              _row(ln_g[i, 2]), _row(ln_b[i, 2]), alpha)
            h = _swap_token_order(h, seq, bsz)
        if i % 2 == 0:
            h = _ffn(h, _Sel(ffn_in, (i, 1)), _Sel(ffn_out, (i, 1)), _row(ln_g[i, 2]), _row(ln_b[i, 2]), alpha)
    return h.reshape(bsz, seq, d)
```

```python
import functools
import math
from typing import NamedTuple

import jax
import jax.numpy as jnp
from jax import lax
from jax.experimental import pallas as pl
from jax.experimental.pallas import tpu as pltpu

F32 = jnp.float32
BF16 = jnp.bfloat16

LN_EPS = 1e-5
GN_EPS = 64e-5
SGU_CHUNK = 128
SGU_GROUP_DIM = 128
RWKV_HEAD = 64

VMEM_LIMIT_BYTES = 56 * 1024 * 1024

FFN_TM = 1024
FFN_SUB = 512
FFN_FC = 256
SGU_TM = 512
SGU_UC = 512
PROJ_TM = 512
SCAN_FFN_TT = 32


def _dot(a, b):
    return jnp.dot(a, b, preferred_element_type=F32)


def _layer_norm(z, g, b, eps=LN_EPS):
    mu = jnp.mean(z, axis=-1, keepdims=True)
    zc = z - mu
    var = jnp.mean(zc * zc, axis=-1, keepdims=True)
    return zc * lax.rsqrt(var + eps) * g + b


def _const_spec(shape):
    nd = len(shape)
    return pl.BlockSpec(shape, lambda i: (0,) * nd, pipeline_mode=pl.Buffered(1))


class _Sel(NamedTuple):
    arr: jax.Array
    idx: tuple

    @property
    def shape(self):
        return self.arr.shape[len(self.idx):]

    def spec(self):
        shape, idx = self.shape, tuple(self.idx)
        return pl.BlockSpec((None,) * len(idx) + shape, lambda i: idx + (0,) * len(shape),
                            pipeline_mode=pl.Buffered(1))


def _params(sem):
    return pltpu.CompilerParams(dimension_semantics=(sem,), vmem_limit_bytes=VMEM_LIMIT_BYTES)


def _ffn_rows(x, win_ref, wout_ref, g_ref, b_ref, act_ref, *, d_ff, fc, alpha):
    xb = x.astype(BF16)
    for c in range(d_ff // fc):
        lo = c * fc
        gate = _dot(xb, win_ref[:, lo:lo + fc])
        up = _dot(xb, win_ref[:, d_ff + lo:d_ff + lo + fc])
        act_ref[:, lo:lo + fc] = (gate * jax.nn.sigmoid(gate) * up).astype(BF16)
    o = _dot(act_ref[...], wout_ref[...])
    return _layer_norm(alpha * x + 0.5 * o, g_ref[...], b_ref[...])


def _ffn_body(x_ref, win_ref, wout_ref, g_ref, b_ref, o_ref, act_ref, *, sub, **kw):
    for s in range(x_ref.shape[0] // sub):
        rows = pl.ds(s * sub, sub)
        o_ref[rows, :] = _ffn_rows(x_ref[rows, :], win_ref, wout_ref, g_ref, b_ref, act_ref.at[rows, :], **kw)


def _ffn(x, w_in, w_out, ln_g, ln_b, alpha, tm=FFN_TM, fc=FFN_FC):
    t, d = x.shape
    d_ff = w_out.shape[0]
    tm = min(tm, t)
    fc = fc if d_ff % fc == 0 else d_ff
    body = functools.partial(_ffn_body, sub=min(FFN_SUB, tm), d_ff=d_ff, fc=fc, alpha=alpha)
    return pl.pallas_call(
        body,
        grid=(t // tm,),
        in_specs=[
            pl.BlockSpec((tm, d), lambda i: (i, 0)),
            w_in.spec(),
            w_out.spec(),
            _const_spec((1, d)),
            _const_spec((1, d)),
        ],
        out_specs=pl.BlockSpec((tm, d), lambda i: (i, 0)),
        out_shape=jax.ShapeDtypeStruct((t, d), F32),
        scratch_shapes=[pltpu.VMEM((tm, d_ff), BF16)],
        compiler_params=_params("parallel"),
        name="ffn",
    )(x, w_in.arr, w_out.arr, ln_g, ln_b)


def _gelu(z):
    return 0.5 * z * (1.0 + lax.erf(z * math.sqrt(0.5)))


def _sgu_body(x_ref, win_ref, bin_ref, lg_ref, lb_ref, wsp_ref, bsp_ref, wout_ref, g_ref, b_ref,
              o_ref, vn_ref, gated_ref, *, d_sgu, uc, alpha):
    tm = x_ref.shape[0]
    x = x_ref[...]
    xb = x.astype(BF16)
    v = _gelu(_dot(xb, win_ref[:, d_sgu:]) + bin_ref[:, d_sgu:])
    vn_ref[...] = _layer_norm(v, lg_ref[...], lb_ref[...]).astype(BF16)
    row = lax.broadcasted_iota(jnp.int32, (SGU_CHUNK, SGU_CHUNK), 0)
    col = lax.broadcasted_iota(jnp.int32, (SGU_CHUNK, SGU_CHUNK), 1)
    causal = col <= row
    groups_per_uc = uc // SGU_GROUP_DIM
    for c in range(d_sgu // uc):
        u = _gelu(_dot(xb, win_ref[:, c * uc:(c + 1) * uc]) + bin_ref[:, c * uc:(c + 1) * uc])
        for gi in range(groups_per_uc):
            g = c * groups_per_uc + gi
            w = jnp.where(causal, wsp_ref[g], 0.0).astype(BF16)
            lanes = slice(g * SGU_GROUP_DIM, (g + 1) * SGU_GROUP_DIM)
            for ci in range(tm // SGU_CHUNK):
                rows = slice(ci * SGU_CHUNK, (ci + 1) * SGU_CHUNK)
                mixed = _dot(w, vn_ref[rows, lanes]) + bsp_ref[g]
                ug = u[rows, gi * SGU_GROUP_DIM:(gi + 1) * SGU_GROUP_DIM]
                gated_ref[rows, lanes] = (ug * mixed).astype(BF16)
    o = _dot(gated_ref[...], wout_ref[...])
    o_ref[...] = _layer_norm(alpha * x + o, g_ref[...], b_ref[...])


def _sgu(x, w_in, b_in, sln_g, sln_b, w_sp, b_sp_bcast, w_out, ln_g, ln_b, alpha, tm=SGU_TM, uc=SGU_UC):
    t, d = x.shape
    d_sgu = w_out.shape[0]
    n_groups = w_sp.shape[0]
    tm = min(tm, t)
    body = functools.partial(_sgu_body, d_sgu=d_sgu, uc=uc, alpha=alpha)
    return pl.pallas_call(
        body,
        grid=(t // tm,),
        in_specs=[
            pl.BlockSpec((tm, d), lambda i: (i, 0)),
            w_in.spec(),
            _const_spec((1, 2 * d_sgu)),
            _const_spec((1, d_sgu)),
            _const_spec((1, d_sgu)),
            _const_spec((n_groups, SGU_CHUNK, SGU_CHUNK)),
            _const_spec((n_groups, SGU_CHUNK, SGU_GROUP_DIM)),
            w_out.spec(),
            _const_spec((1, d)),
            _const_spec((1, d)),
        ],
        out_specs=pl.BlockSpec((tm, d), lambda i: (i, 0)),
        out_shape=jax.ShapeDtypeStruct((t, d), F32),
        scratch_shapes=[pltpu.VMEM((tm, d_sgu), BF16), pltpu.VMEM((tm, d_sgu), BF16)],
        compiler_params=_params("parallel"),
        name="sgu",
    )(x, w_in.arr, b_in, sln_g, sln_b, w_sp, b_sp_bcast, w_out.arr, ln_g, ln_b)


def _softplus(z):
    return jnp.maximum(z, 0.0) + jnp.log1p(jnp.exp(-jnp.abs(z)))


def _proj_body(*refs, has_vres):
    if has_vres:
        (x_ref, xp_ref, mu_ref, wr_ref, wk_ref, wv_ref, w0_ref, w1_ref, w2_ref, a0_ref, a1_ref, a2_ref,
         g1_ref, g2_ref, vf_ref, v0_ref, v1_ref, v2_ref,
         r_out, w_out, k_out, v_out, a_out, g_out) = refs
    else:
        (x_ref, xp_ref, mu_ref, wr_ref, wk_ref, wv_ref, w0_ref, w1_ref, w2_ref, a0_ref, a1_ref, a2_ref,
         g1_ref, g2_ref,
         r_out, w_out, k_out, v_out, a_out, g_out) = refs
    bsz = xp_ref.shape[0]
    x = x_ref[...]
    prev = jnp.where(pl.program_id(0) == 0, 0.0, xp_ref[...])
    xx = jnp.concatenate([prev, x[:-bsz]], axis=0) - x

    def mix(n):
        return (x + xx * mu_ref[n:n + 1, :]).astype(BF16)

    r_out[...] = _dot(mix(0), wr_ref[...])
    k_out[...] = _dot(mix(1), wk_ref[...])
    xv = mix(2)
    v = _dot(xv, wv_ref[...])
    if has_vres:
        gate = jax.nn.sigmoid(v0_ref[...] + _dot(_dot(xv, v1_ref[...]).astype(BF16), v2_ref[...]))
        v = v + (vf_ref[...] - v) * gate
    v_out[...] = v
    lw = w0_ref[...] + _dot(jnp.tanh(_dot(mix(3), w1_ref[...])).astype(BF16), w2_ref[...])
    w_out[...] = -_softplus(-lw) - 0.5
    a_out[...] = jax.nn.sigmoid(a0_ref[...] + _dot(_dot(mix(4), a1_ref[...]).astype(BF16), a2_ref[...]))
    g_out[...] = _dot(jax.nn.sigmoid(_dot(mix(5), g1_ref[...])).astype(BF16), g2_ref[...])


def _rwkv_proj(x, bsz, mu, w_rkv, w0, w1, w2, a0, a1, a2, g1, g2, vres, tm=PROJ_TM):
    t, d = x.shape
    tm = min(tm, t)
    assert tm % bsz == 0 and bsz % 8 == 0
    has_vres = vres is not None
    tile = pl.BlockSpec((tm, d), lambda i: (i, 0))
    prev_rows = pl.BlockSpec((bsz, d), lambda i: (jnp.maximum(i * (tm // bsz) - 1, 0), 0))
    small = [w0, w1, w2, a0, a1, a2, g1, g2]
    args = [x, x, mu] + [w.arr for w in w_rkv] + small
    specs = ([tile, prev_rows, _const_spec(mu.shape)] + [w.spec() for w in w_rkv]
             + [_const_spec(a.shape) for a in small])
    if has_vres:
        v_first, v0, v1, v2 = vres
        args += [v_first, v0, v1, v2]
        specs += [tile, _const_spec(v0.shape), _const_spec(v1.shape), _const_spec(v2.shape)]
    body = functools.partial(_proj_body, has_vres=has_vres)
    return pl.pallas_call(
        body,
        grid=(t // tm,),
        in_specs=specs,
        out_specs=[tile] * 6,
        out_shape=[jax.ShapeDtypeStruct((t, d), F32)] * 6,
        compiler_params=_params("parallel"),
        name="rwkv_proj",
    )(*args)


_AV, _RW, _DEC, _BV, _K2, _V = range(6)
_C_B, _C_K, _C_BONUS = range(3)


def _scan_steps(r_ref, w_ref, k_ref, v_ref, a_ref, kk_ref, ka_ref, rk_ref, gg_ref, gb_ref,
                o_ref, s_ref, ops_ref, crow_ref):
    n = RWKV_HEAD
    half = 64
    n_pairs = r_ref.shape[1] // (2 * n)
    bsz = half // n_pairs
    tt = r_ref.shape[0] // bsz
    lane = lax.broadcasted_iota(jnp.int32, (n, 2 * half), 1)
    low = lane < half

    def row0(t):
        return t * bsz if isinstance(t, int) else pl.multiple_of(t * bsz, bsz)

    def load_step(x_ref, t):
        rows = x_ref[pl.ds(row0(t), bsz), :]
        return jnp.concatenate([rows[:, hp * 2 * n:(hp + 1) * 2 * n] for hp in range(n_pairs)], axis=0)

    def load_pair(x_ref, t):
        tr = jnp.concatenate([load_step(x_ref, t), load_step(x_ref, t + 1)], axis=0).T
        top, bot = tr[:n], tr[n:]
        return (jnp.where(low, top, pltpu.roll(bot, half, axis=1)),
                jnp.where(low, pltpu.roll(top, half, axis=1), bot))

    def prep(t, slot):
        rs, ws, ks, vs, gs = (load_pair(x, t) for x in (r_ref, w_ref, k_ref, v_ref, a_ref))
        for st in range(2):
            r, k, v, a = rs[st], ks[st], vs[st], gs[st]
            kk = k * kk_ref[...]
            norm = jnp.sqrt(jnp.sum(kk * kk, axis=0, keepdims=True))
            kk = kk / jnp.maximum(norm, 1e-12)
            k2 = k * (1.0 + (a - 1.0) * ka_ref[...])
            dec = jnp.exp(-jnp.exp(ws[st]))
            bv = kk * a
            rk2 = r * k2
            ops_ref[slot, st, _AV] = -kk
            ops_ref[slot, st, _RW] = r * dec
            ops_ref[slot, st, _DEC] = dec
            ops_ref[slot, st, _BV] = bv
            ops_ref[slot, st, _K2] = k2
            ops_ref[slot, st, _V] = v
            crow_ref[slot, st, _C_B:_C_B + 1] = jnp.sum(r * bv, axis=0, keepdims=True)
            crow_ref[slot, st, _C_K:_C_K + 1] = jnp.sum(rk2, axis=0, keepdims=True)
            crow_ref[slot, st, _C_BONUS:_C_BONUS + 1] = jnp.sum(rk2 * rk_ref[...], axis=0, keepdims=True)

    def row(slot, st, which, j):
        return ops_ref[slot, st, which, pl.ds(j, 1), :]

    def reduce_only(slot, st):
        sa = jnp.zeros((n, 2 * half), F32)
        ya = jnp.zeros((n, 2 * half), F32)
        for j in range(n):
            s = s_ref[j]
            sa = sa + s * row(slot, st, _AV, j)
            ya = ya + s * row(slot, st, _RW, j)
        return sa, ya

    def advance(sa, ya, upd, nxt):
        us, ut = upd
        v = ops_ref[us, ut, _V]
        y = (ya + sa * crow_ref[us, ut, _C_B:_C_B + 1] + v * crow_ref[us, ut, _C_K:_C_K + 1])
        mean = jnp.mean(y, axis=0, keepdims=True)
        yc = y - mean
        var = jnp.mean(yc * yc, axis=0, keepdims=True)
        out = (yc * lax.rsqrt(var + GN_EPS) * gg_ref[...] + gb_ref[...]
               + crow_ref[us, ut, _C_BONUS:_C_BONUS + 1] * v)
        sa_n = None if nxt is None else jnp.zeros_like(sa)
        ya_n = None if nxt is None else jnp.zeros_like(ya)
        for j in range(n):
            s = s_ref[j] * row(us, ut, _DEC, j) + sa * row(us, ut, _BV, j) + v * row(us, ut, _K2, j)
            s_ref[j] = s
            if nxt is not None:
                sa_n = sa_n + s * row(nxt[0], nxt[1], _AV, j)
                ya_n = ya_n + s * row(nxt[0], nxt[1], _RW, j)
        return out, sa_n, ya_n

    def store_pair(t, y0, y1):
        top = jnp.where(low, y0, pltpu.roll(y1, half, axis=1))
        bot = jnp.where(low, pltpu.roll(y0, half, axis=1), y1)
        back = jnp.concatenate([top, bot], axis=0).T
        for st in range(2):
            rows = jnp.concatenate(
                [back[st * half + hp * bsz:st * half + (hp + 1) * bsz] for hp in range(n_pairs)], axis=1)
            o_ref[pl.ds(row0(t + st), bsz), :] = rows

    def quad(t, sa, ya, last):
        prep(t + 2, 1)
        y0, sa, ya = advance(sa, ya, (0, 0), (0, 1))
        y1, sa, ya = advance(sa, ya, (0, 1), (1, 0))
        store_pair(t, y0, y1)
        if not last:
            prep(t + 4, 0)
        y2, sa, ya = advance(sa, ya, (1, 0), (1, 1))
        y3, sa, ya = advance(sa, ya, (1, 1), None if last else (0, 0))
        store_pair(t + 2, y2, y3)
        return sa, ya

    return prep, reduce_only, quad, tt


def _scan_ffn_body(r_ref, w_ref, k_ref, v_ref, a_ref, kk_ref, ka_ref, rk_ref, gg_ref, gb_ref,
                   x_ref, g_ref, wo_ref, lg1_ref, lb1_ref, win_ref, wout_ref, lg2_ref, lb2_ref,
                   o_ref, s_ref, ops_ref, crow_ref, ybuf_ref, h_ref, hb_ref, act_ref, o2_ref, *, d_ff, alpha):
    i = pl.program_id(0)

    @pl.when(i == 0)
    def _():
        s_ref[...] = jnp.zeros_like(s_ref)
        ybuf_ref[...] = jnp.zeros_like(ybuf_ref)

    cur = lax.rem(i, jnp.int32(2))
    prep, reduce_only, quad, tt = _scan_steps(r_ref, w_ref, k_ref, v_ref, a_ref, kk_ref, ka_ref, rk_ref,
                                              gg_ref, gb_ref, ybuf_ref.at[cur], s_ref, ops_ref, crow_ref)
    y_prev = ybuf_ref.at[1 - cur]
    d = x_ref.shape[1]

    def f_out():
        yg = (y_prev[...] * g_ref[...]).astype(BF16)
        h = _layer_norm(alpha * x_ref[...] + _dot(yg, wo_ref[...]), lg1_ref[...], lb1_ref[...])
        h_ref[...] = h
        hb_ref[...] = h.astype(BF16)

    def f_act(lo, hi):
        hb = hb_ref[...]
        gate = _dot(hb, win_ref[:, lo:hi])
        up = _dot(hb, win_ref[:, d_ff + lo:d_ff + hi])
        act_ref[:, lo:hi] = (gate * jax.nn.sigmoid(gate) * up).astype(BF16)

    def f_down(lo, hi):
        o2_ref[:, lo:hi] = _dot(act_ref[...], wout_ref[:, lo:hi])

    def f_ln():
        o_ref[...] = _layer_norm(alpha * h_ref[...] + 0.5 * o2_ref[...], lg2_ref[...], lb2_ref[...])

    lane_tile = 128
    cuts = [round(c * (d_ff // lane_tile) / 4) * lane_tile for c in range(5)]
    slices = ([f_out] + [functools.partial(f_act, cuts[c], cuts[c + 1]) for c in range(4)]
              + [functools.partial(f_down, 0, d // 2), functools.partial(f_down, d // 2, d), f_ln])
    n_quads = tt // 4
    assert n_quads >= 1
    per_quad = -(-len(slices) // n_quads)

    slices.pop(0)()
    prep(0, 0)
    sa, ya = reduce_only(0, 0)
    for q in range(n_quads):
        sa, ya = quad(q * 4, sa, ya, q == n_quads - 1)
        take = len(slices) if q == n_quads - 1 else min(per_quad, len(slices))
        for _ in range(take):
            slices.pop(0)()


def _rwkv_scan_ffn(r, w, k, v, a, bsz, kk_p, ka_p, rk_p, gg_p, gb_p, x, g, w_o, ln_g, ln_b,
                   w_in, w_out, fln_g, fln_b, alpha, tt=SCAN_FFN_TT):
    rows, d = r.shape
    s = rows // bsz
    tt = min(tt, s)
    n = RWKV_HEAD
    lanes = 2 * n
    d_ff = w_out.shape[0]
    assert tt % 4 == 0 and bsz % 8 == 0 and bsz * (d // lanes) == 64 and d_ff % 128 == 0
    n_tiles = s // tt
    tr = tt * bsz
    cur_tile = pl.BlockSpec((tr, d), lambda i: (jnp.minimum(i, n_tiles - 1), 0))
    prev_tile = pl.BlockSpec((tr, d), lambda i: (jnp.maximum(i - 1, 0), 0))
    par = _const_spec((n, lanes))
    vec = _const_spec((1, d))
    body = functools.partial(_scan_ffn_body, d_ff=d_ff, alpha=alpha)
    return pl.pallas_call(
        body,
        grid=(n_tiles + 1,),
        in_specs=[cur_tile] * 5 + [par] * 5 + [prev_tile, prev_tile, w_o.spec(), vec, vec,
                                                w_in.spec(), w_out.spec(), vec, vec],
        out_specs=prev_tile,
        out_shape=jax.ShapeDtypeStruct((rows, d), F32),
        scratch_shapes=[pltpu.VMEM((n, n, lanes), F32),
                        pltpu.VMEM((2, 2, 6, n, lanes), F32),
                        pltpu.VMEM((2, 2, 8, lanes), F32),
                        pltpu.VMEM((2, tr, d), F32),
                        pltpu.VMEM((tr, d), F32),
                        pltpu.VMEM((tr, d), BF16),
                        pltpu.VMEM((tr, d_ff), BF16),
                        pltpu.VMEM((tr, d), F32)],
        compiler_params=_params("arbitrary"),
        name="rwkv_scan_ffn",
    )(r, w, k, v, a, kk_p, ka_p, rk_p, gg_p, gb_p, x, g, w_o.arr, ln_g, ln_b, w_in.arr, w_out.arr, fln_g, fln_b)


def _param_inst(p, bsz, n_heads):
    n_pairs = n_heads // 2
    t = p.reshape(n_pairs, 2, RWKV_HEAD).transpose(2, 1, 0)
    t = jnp.broadcast_to(t[:, :, :, None], (RWKV_HEAD, 2, n_pairs, bsz))
    return t.reshape(RWKV_HEAD, 2 * n_pairs * bsz)


def _swap_token_order(h, rows_major, rows_minor):
    d = h.shape[-1]
    return h.reshape(rows_major, rows_minor, d).transpose(1, 0, 2).reshape(rows_major * rows_minor, d)


def _row(p):
    return p.reshape(1, -1)


def kernel(x, ln_g, ln_b, ffn_w_in, ffn_w_out, sgu_w_in, sgu_b_in, sgu_ln_g, sgu_ln_b, sgu_w_sp, sgu_b_sp, sgu_w_out, rwkv_mu, rwkv_w_rkv, rwkv_w0, rwkv_w1, rwkv_w2, rwkv_a0, rwkv_a1, rwkv_a2, rwkv_v0, rwkv_v1, rwkv_v2, rwkv_g1, rwkv_g2, rwkv_k_k, rwkv_k_a, rwkv_r_k, rwkv_gn_g, rwkv_gn_b, rwkv_w_o):
    bsz, seq, d = x.shape
    depth = ln_g.shape[0]
    n_heads = d // RWKV_HEAD
    alpha = (2 * depth) ** 0.25
    h = x.reshape(bsz * seq, d)
    ffn_in, ffn_out = ffn_w_in.astype(BF16), ffn_w_out.astype(BF16)
    sgu_in, sgu_out = sgu_w_in.astype(BF16), sgu_w_out.astype(BF16)
    w_rkv_b, w_o_b = rwkv_w_rkv.astype(BF16), rwkv_w_o.astype(BF16)
    v_first = None
    for i in range(depth):
        j = i // 2
        if i % 2 == 1:
            h = _swap_token_order(h, bsz, seq)
        h = _ffn(h, _Sel(ffn_in, (i, 0)), _Sel(ffn_out, (i, 0)), _row(ln_g[i, 0]), _row(ln_b[i, 0]), alpha)
        if i % 2 == 0:
            b_sp = jnp.broadcast_to(sgu_b_sp[j][:, :, None], sgu_b_sp[j].shape + (SGU_GROUP_DIM,))
            h = _sgu(h, _Sel(sgu_in, (j,)), _row(sgu_b_in[j]), _row(sgu_ln_g[j]), _row(sgu_ln_b[j]),
                     sgu_w_sp[j], b_sp, _Sel(sgu_out, (j,)), _row(ln_g[i, 1]), _row(ln_b[i, 1]), alpha)
        else:
            vres = None
            if v_first is not None:
                vres = (v_first, _row(rwkv_v0[j - 1]), rwkv_v1[j - 1].astype(BF16), rwkv_v2[j - 1].astype(BF16))
            r, w, k, v, a, g = _rwkv_proj(
                h, bsz, rwkv_mu[j], [_Sel(w_rkv_b, (j, n)) for n in range(3)], _row(rwkv_w0[j]),
                rwkv_w1[j].astype(BF16), rwkv_w2[j].astype(BF16), _row(rwkv_a0[j]),
                rwkv_a1[j].astype(BF16), rwkv_a2[j].astype(BF16),
                rwkv_g1[j].astype(BF16), rwkv_g2[j].astype(BF16), vres)
            if v_first is None:
                v_first = v
            pi = functools.partial(_param_inst, bsz=bsz, n_heads=n_heads)
            h = _rwkv_scan_ffn(r, w, k, v, a, bsz, pi(rwkv_k_k[j]), pi(rwkv_k_a[j]),
                               pi(rwkv_r_k[j].reshape(-1)), pi(rwkv_gn_g[j]), pi(rwkv_gn_b[j]),
                               h, g, _Sel(w_o_b, (j,)), _row(ln_g[i, 1]), _row(ln_b[i, 1]),
                               _Sel(ffn_in, (i, 1)), _Sel(ffn_out, (i, 1)),
                               _row(ln_g[i, 2]), _row(ln_b[i, 2]), alpha)
            h = _swap_token_order(h, seq, bsz)
        if i % 2 == 0:
            h = _ffn(h, _Sel(ffn_in, (i, 1)), _Sel(ffn_out, (i, 1)), _row(ln_g[i, 2]), _row(ln_b[i, 2]), alpha)
    return h.reshape(bsz, seq, d)
```
